```python
import jax, jax.numpy as jnp
from jax import lax
import numpy as np

D_MODEL = 4096
BATCH = 1
SEQ = 16384
DEPTH = 4

CHUNK = 64
GMLP_BLOCK = 128
GMLP_WIDTH = D_MODEL // 2
GMLP_GROUP_DIM = 128
GMLP_GROUPS = GMLP_WIDTH // GMLP_GROUP_DIM
DN_WIDTH = D_MODEL // 2
DN_HEAD_DIM = 128
DN_HEADS = DN_WIDTH // DN_HEAD_DIM
CONV_WIDTH = 4
NORM_EPS = 1e-6
IN_SIZES = (GMLP_WIDTH, GMLP_WIDTH, GMLP_WIDTH, 3 * DN_WIDTH, DN_WIDTH, DN_HEADS, DN_HEADS, D_MODEL, D_MODEL)
N_IN_COLS = 3 * GMLP_WIDTH + 4 * DN_WIDTH + 2 * DN_HEADS + 2 * D_MODEL

kernel_name = "hybrid_gmlp_gated_deltanet_trunk"


def rms_norm(x, g):
    xf = x.astype(jnp.float32)
    y = xf * lax.rsqrt(jnp.mean(xf * xf, axis=-1, keepdims=True) + NORM_EPS)
    return (y * g.astype(jnp.float32)).astype(x.dtype)


def layer_norm(x, g, b):
    xf = x.astype(jnp.float32)
    mu = jnp.mean(xf, axis=-1, keepdims=True)
    xc = xf - mu
    y = xc * lax.rsqrt(jnp.mean(xc * xc, axis=-1, keepdims=True) + NORM_EPS)
    return (y * g.astype(jnp.float32) + b.astype(jnp.float32)).astype(x.dtype)


def l2_normalize(x):
    return x * lax.rsqrt(jnp.sum(x * x, axis=-1, keepdims=True) + NORM_EPS)


def split_columns(proj):
    parts, start = [], 0
    for size in IN_SIZES:
        parts.append(proj[..., start:start + size])
        start += size
    return parts


def causal_depthwise_conv(x, w):
    c = x.shape[-1]
    return lax.conv_general_dilated(
        x, w[:, None, :], window_strides=(1,), padding=[(CONV_WIDTH - 1, 0)],
        dimension_numbers=("NWC", "WIO", "NWC"), feature_group_count=c)


def gmlp_spatial_gating(u, v, ln_g, ln_b, w_s, b_s):
    bsz, seq, _ = u.shape
    nb = seq // GMLP_BLOCK
    v = layer_norm(v, ln_g, ln_b)
    chunk_id = jnp.arange(GMLP_BLOCK) // CHUNK
    mask = chunk_id[None, :] <= chunk_id[:, None]
    ws = jnp.where(mask[None], w_s, jnp.zeros((), w_s.dtype))
    vb = v.reshape(bsz, nb, GMLP_BLOCK, GMLP_GROUPS, GMLP_GROUP_DIM)
    mixed = jnp.einsum("gpq,bnqgc->bnpgc", ws, vb) + b_s.T[None, None, :, :, None]
    return u * mixed.reshape(bsz, seq, GMLP_WIDTH)


def to_chunks(t):
    bsz, seq = t.shape[0], t.shape[1]
    t = t.reshape(bsz, seq // CHUNK, CHUNK, *t.shape[2:])
    return jnp.moveaxis(t, 2, 3)


def gated_delta_rule(q, k, v, beta, g):
    bsz, seq, nh, dk = q.shape
    dv = v.shape[-1]
    qc, kc, vc = to_chunks(q), to_chunks(k), to_chunks(v)
    bc, gcum = to_chunks(beta), jnp.cumsum(to_chunks(g), axis=-1)
    idx = jnp.arange(CHUNK)
    incl = idx[:, None] >= idx[None, :]
    strict = idx[:, None] > idx[None, :]
    decay = jnp.exp(jnp.where(incl, gcum[..., :, None] - gcum[..., None, :], -jnp.inf))
    kk = jnp.einsum("bnhid,bnhjd->bnhij", kc, kc)
    lmat = jnp.where(strict, bc[..., :, None] * kk * decay, 0.0)
    gam = jnp.exp(gcum)
    rhs = jnp.concatenate([bc[..., None] * vc, (bc * gam)[..., None] * kc], axis=-1)
    sol = lax.linalg.triangular_solve(jnp.eye(CHUNK, dtype=jnp.float32) + lmat, rhs,
                                      left_side=True, lower=True, unit_diagonal=True)
    u_new, k_cum = sol[..., :dv], sol[..., dv:]
    aqk = jnp.einsum("bnhid,bnhjd->bnhij", qc, kc) * decay
    q_dec = qc * gam[..., None]
    k_dec = kc * jnp.exp(gcum[..., -1:] - gcum)[..., None]
    g_end = gam[..., -1]

    def step(state, inp):
        u_i, kcum_i, aqk_i, qd_i, kd_i, ge_i = inp
        w = u_i - jnp.einsum("bhcd,bhde->bhce", kcum_i, state)
        o = jnp.einsum("bhcd,bhde->bhce", qd_i, state) + jnp.einsum("bhij,bhje->bhie", aqk_i, w)
        state = ge_i[..., None, None] * state + jnp.einsum("bhcd,bhce->bhde", kd_i, w)
        return state, o

    xs = tuple(jnp.moveaxis(t, 1, 0) for t in (u_new, k_cum, aqk, q_dec, k_dec, g_end))
    state0 = jnp.zeros((bsz, nh, dk, dv), jnp.float32)
    _, o = lax.scan(step, state0, xs)
    o = jnp.swapaxes(jnp.moveaxis(o, 0, 1), 2, 3)
    return o.reshape(bsz, seq, nh, dv)


def setup_inputs(seed: int = 0) -> dict:
    key = jax.random.key(seed)
    ks = jax.random.split(key, 20)
    f32 = jnp.float32
    nrm = lambda k, shape, scale: jax.random.normal(k, shape, f32) * scale
    x = nrm(ks[0], (BATCH, SEQ, D_MODEL), 1.0)
    norm_g = 1.0 + nrm(ks[1], (DEPTH, D_MODEL), 0.02)
    w_in = nrm(ks[2], (DEPTH, D_MODEL, N_IN_COLS), D_MODEL ** -0.5)
    conv_w = nrm(ks[3], (DEPTH, CONV_WIDTH, 3 * DN_WIDTH), CONV_WIDTH ** -0.5)
    a_log = jnp.log(jax.random.uniform(ks[4], (DEPTH, DN_HEADS), f32, 1.0, 16.0))
    dt = jnp.exp(jax.random.uniform(ks[5], (DEPTH, DN_HEADS), f32, np.log(1e-3), np.log(1e-1)))
    dt_bias = dt + jnp.log(-jnp.expm1(-dt))
    dn_norm_g = 1.0 + nrm(ks[6], (DEPTH, DN_HEAD_DIM), 0.02)
    ln_g = 1.0 + nrm(ks[7], (DEPTH, GMLP_WIDTH), 0.02)
    ln_b = nrm(ks[8], (DEPTH, GMLP_WIDTH), 0.02)
    w_s = nrm(ks[9], (DEPTH, GMLP_GROUPS, GMLP_BLOCK, GMLP_BLOCK), GMLP_BLOCK ** -0.5)
    b_s = 1.0 + nrm(ks[10], (DEPTH, GMLP_GROUPS, GMLP_BLOCK), 0.1)
    w_br_gmlp = nrm(ks[11], (DEPTH, GMLP_WIDTH, D_MODEL), GMLP_WIDTH ** -0.5)
    w_br_dn = nrm(ks[12], (DEPTH, DN_WIDTH, D_MODEL), DN_WIDTH ** -0.5)
    w_out = nrm(ks[13], (DEPTH, D_MODEL, D_MODEL), D_MODEL ** -0.5)
    final_g = 1.0 + nrm(ks[14], (D_MODEL,), 0.02)
    return {"x": x, "norm_g": norm_g, "w_in": w_in, "conv_w": conv_w, "a_log": a_log,
            "dt_bias": dt_bias, "dn_norm_g": dn_norm_g, "ln_g": ln_g, "ln_b": ln_b,
            "w_s": w_s, "b_s": b_s, "w_br_gmlp": w_br_gmlp, "w_br_dn": w_br_dn,
            "w_out": w_out, "final_g": final_g}


def reference(x, norm_g, w_in, conv_w, a_log, dt_bias, dn_norm_g, ln_g, ln_b, w_s, b_s,
              w_br_gmlp, w_br_dn, w_out, final_g):
    bsz, seq, _ = x.shape
    for l in range(DEPTH):
        h = rms_norm(x, norm_g[l])
        proj = h @ w_in[l]
        u_pre, v_pre, z_gm, qkv_pre, z_dn, b_pre, a_pre, gate_gm, gate_dn = split_columns(proj)

        y_gm = gmlp_spatial_gating(jax.nn.gelu(u_pre, approximate=False),
                                   jax.nn.gelu(v_pre, approximate=False),
                                   ln_g[l], ln_b[l], w_s[l], b_s[l])
        y_gm = (y_gm * jax.nn.silu(z_gm)) @ w_br_gmlp[l]

        qkv = jax.nn.silu(causal_depthwise_conv(qkv_pre, conv_w[l]))
        hs = (bsz, seq, DN_HEADS, DN_HEAD_DIM)
        q = l2_normalize(qkv[..., :DN_WIDTH].reshape(hs).astype(jnp.float32)) * (DN_HEAD_DIM ** -0.5)
        k = l2_normalize(qkv[..., DN_WIDTH:2 * DN_WIDTH].reshape(hs).astype(jnp.float32))
        v = qkv[..., 2 * DN_WIDTH:].reshape(hs).astype(jnp.float32)
        beta = jax.nn.sigmoid(b_pre.astype(jnp.float32))
        g = -jnp.exp(a_log[l].astype(jnp.float32)) * jax.nn.softplus(
            a_pre.astype(jnp.float32) + dt_bias[l].astype(jnp.float32))
        o = gated_delta_rule(q, k, v, beta, g)
        o = rms_norm(o, dn_norm_g[l]).astype(x.dtype).reshape(bsz, seq, DN_WIDTH)
        y_dn = (o * jax.nn.silu(z_dn)) @ w_br_dn[l]

        merged = jax.nn.sigmoid(gate_gm) * y_gm + jax.nn.sigmoid(gate_dn) * y_dn
        x = x + merged @ w_out[l]
    return rms_norm(x, final_g)
```

```python
import functools

import jax
import jax.numpy as jnp
from jax import lax
from jax.experimental import pallas as pl
from jax.experimental.pallas import tpu as pltpu

D_MODEL = 4096
DEPTH = 4
CHUNK = 64
GMLP_BLOCK = 128
GMLP_WIDTH = D_MODEL // 2
GMLP_GROUP_DIM = 128
GMLP_GROUPS = GMLP_WIDTH // GMLP_GROUP_DIM
DN_WIDTH = D_MODEL // 2
DN_HEAD_DIM = 128
DN_HEADS = DN_WIDTH // DN_HEAD_DIM
CONV_WIDTH = 4
NORM_EPS = 1e-6
LANES = 128
SUBLANES = 8
N_MAIN = 3 * GMLP_WIDTH + 4 * DN_WIDTH
N_BA = 2 * DN_HEADS
VMEM_LIMIT = 56 * 1024 * 1024

HIGHEST = lax.Precision.HIGHEST


def _params(*sem):
    return pltpu.CompilerParams(dimension_semantics=sem, vmem_limit_bytes=VMEM_LIMIT)


def _sigmoid(x):
    return 1.0 / (1.0 + jnp.exp(-x))


def _silu(x):
    return x * _sigmoid(x)


def _gelu(x):
    return 0.5 * x * (1.0 + lax.erf(x * (2.0 ** -0.5)))


def _rmsnorm_kernel(x_ref, g_ref, o_ref):
    x = x_ref[...]
    y = x * lax.rsqrt(jnp.mean(x * x, axis=-1, keepdims=True) + NORM_EPS)
    o_ref[...] = (y * g_ref[...]).astype(o_ref.dtype)


def _rmsnorm(x, g, out_dtype, tr=512):
    s, d = x.shape
    return pl.pallas_call(
        _rmsnorm_kernel,
        grid=(s // tr,),
        in_specs=[pl.BlockSpec((tr, d), lambda i: (i, 0)),
                  pl.BlockSpec((1, d), lambda i: (0, 0))],
        out_specs=pl.BlockSpec((tr, d), lambda i: (i, 0)),
        out_shape=jax.ShapeDtypeStruct((s, d), out_dtype),
        compiler_params=_params("arbitrary"),
        name="rmsnorm",
    )(x, g.reshape(1, d))


def _matmul_kernel(a_ref, b_ref, o_ref):
    o_ref[...] = jnp.dot(a_ref[...], b_ref[...],
                         preferred_element_type=jnp.float32).astype(o_ref.dtype)


def _matmul(a, b, out_dtype, tm, tn, name):
    m, k = a.shape
    n = b.shape[1]
    return pl.pallas_call(
        _matmul_kernel,
        grid=(m // tm, n // tn),
        in_specs=[pl.BlockSpec((tm, k), lambda i, j: (i, 0)),
                  pl.BlockSpec((k, tn), lambda i, j: (0, j))],
        out_specs=pl.BlockSpec((tm, tn), lambda i, j: (i, j)),
        out_shape=jax.ShapeDtypeStruct((m, n), out_dtype),
        compiler_params=_params("arbitrary", "arbitrary"),
        name=name,
    )(a, b)


def _gmlp_kernel(u_ref, v_ref, z_ref, lng_ref, lnb_ref, ws_ref, bexp_ref, o_ref):
    u = _gelu(u_ref[...])
    v = _gelu(v_ref[...])
    mu = jnp.mean(v, axis=-1, keepdims=True)
    vc = v - mu
    vn = vc * lax.rsqrt(jnp.mean(vc * vc, axis=-1, keepdims=True) + NORM_EPS)
    vn = vn * lng_ref[...] + lnb_ref[...]
    row = lax.broadcasted_iota(jnp.int32, (GMLP_BLOCK, GMLP_BLOCK), 0) // CHUNK
    col = lax.broadcasted_iota(jnp.int32, (GMLP_BLOCK, GMLP_BLOCK), 1) // CHUNK
    mask = col <= row
    gate = _silu(z_ref[...])
    for g in range(GMLP_GROUPS):
        sl = slice(g * GMLP_GROUP_DIM, (g + 1) * GMLP_GROUP_DIM)
        ws = jnp.where(mask, ws_ref[g], 0.0)
        mixed = jnp.dot(ws, vn[:, sl], preferred_element_type=jnp.float32) + bexp_ref[:, sl]
        o_ref[:, sl] = (u[:, sl] * mixed * gate[:, sl]).astype(o_ref.dtype)


def _gmlp(proj, ln_g, ln_b, w_s, bexp):
    s = proj.shape[0]
    w = GMLP_WIDTH
    row = lambda c: pl.BlockSpec((GMLP_BLOCK, w), lambda i, c=c: (i, c))
    full2 = lambda shp: pl.BlockSpec(shp, lambda i: (0, 0))
    return pl.pallas_call(
        _gmlp_kernel,
        grid=(s // GMLP_BLOCK,),
        in_specs=[row(0), row(1), row(2), full2((1, w)), full2((1, w)),
                  pl.BlockSpec((GMLP_GROUPS, GMLP_BLOCK, GMLP_BLOCK), lambda i: (0, 0, 0)),
                  full2((GMLP_BLOCK, w))],
        out_specs=pl.BlockSpec((GMLP_BLOCK, w), lambda i: (i, 0)),
        out_shape=jax.ShapeDtypeStruct((s, w), jnp.bfloat16),
        compiler_params=_params("arbitrary"),
        name="gmlp_gating",
    )(proj, proj, proj, ln_g.reshape(1, w), ln_b.reshape(1, w), w_s, bexp)


def _heads(x2d):
    return jnp.stack([x2d[:, h * DN_HEAD_DIM:(h + 1) * DN_HEAD_DIM] for h in range(DN_HEADS)], axis=0)


def _bmm(a, b, precision=None):
    return jnp.einsum("hij,hjk->hik", a, b, preferred_element_type=jnp.float32, precision=precision)


def _bmm_nt(a, b, precision=None):
    return jnp.einsum("hid,hjd->hij", a, b, preferred_element_type=jnp.float32, precision=precision)


def _deltanet_kernel(qkv_ref, ba_ref, z_ref, convw_ref, alog_ref, dtb_ref, ng_ref, o_ref,
                     cbuf, gct_ref, state_ref):
    f32 = jnp.float32
    i = pl.program_id(0)
    halo = SUBLANES

    @pl.when(i == 0)
    def _():
        cbuf[0:halo, :] = jnp.zeros((halo, 3 * DN_WIDTH), f32)
        state_ref[...] = jnp.zeros(state_ref.shape, f32)

    cbuf[halo:halo + CHUNK, :] = qkv_ref[...]
    conv = jnp.zeros((CHUNK, 3 * DN_WIDTH), f32)
    for t in range(CONV_WIDTH):
        off = halo - (CONV_WIDTH - 1) + t
        conv = conv + convw_ref[t:t + 1, :] * cbuf[off:off + CHUNK, :]
    cbuf[0:halo, :] = cbuf[CHUNK:CHUNK + halo, :]
    qkv = _silu(conv)

    q = _heads(qkv[:, :DN_WIDTH])
    k = _heads(qkv[:, DN_WIDTH:2 * DN_WIDTH])
    v = _heads(qkv[:, 2 * DN_WIDTH:])
    q = q * lax.rsqrt(jnp.sum(q * q, axis=-1, keepdims=True) + NORM_EPS) * (DN_HEAD_DIM ** -0.5)
    k = k * lax.rsqrt(jnp.sum(k * k, axis=-1, keepdims=True) + NORM_EPS)

    ba = ba_ref[...]
    beta_all = _sigmoid(ba)
    xg = ba + dtb_ref[...]
    softplus = jnp.maximum(xg, 0.0) + jnp.log1p(jnp.exp(-jnp.abs(xg)))
    g_all = -jnp.exp(alog_ref[...]) * softplus
    r = lax.broadcasted_iota(jnp.int32, (CHUNK, CHUNK), 0)
    c = lax.broadcasted_iota(jnp.int32, (CHUNK, CHUNK), 1)
    incl = r >= c
    strict = r > c
    gcum_all = jnp.dot(incl.astype(f32), g_all, preferred_element_type=f32, precision=HIGHEST)

    er = lax.broadcasted_iota(jnp.int32, (LANES, DN_WIDTH), 0)
    ec = lax.broadcasted_iota(jnp.int32, (LANES, DN_WIDTH), 1) // DN_HEAD_DIM
    exp_b = (er == ec).astype(f32)
    exp_g = (er == ec + DN_HEADS).astype(f32)
    b3 = _heads(jnp.dot(beta_all, exp_b, preferred_element_type=f32, precision=HIGHEST))
    gc3 = _heads(jnp.dot(gcum_all, exp_g, preferred_element_type=f32, precision=HIGHEST))

    eye = (lax.broadcasted_iota(jnp.int32, (LANES, LANES), 0)
           == lax.broadcasted_iota(jnp.int32, (LANES, LANES), 1)).astype(f32)
    gct_ref[...] = lax.dot_general(eye, gcum_all, (((1,), (1,)), ((), ())),
                                   preferred_element_type=f32, precision=HIGHEST)
    gcr3 = jnp.stack([gct_ref[DN_HEADS + h:DN_HEADS + h + 1, :] for h in range(DN_HEADS)], axis=0)

    decay = jnp.where(incl[None], jnp.exp(gc3[:, :, :CHUNK] - gcr3), 0.0)
    gam = jnp.exp(gc3)
    gc_last = gc3[:, CHUNK - 1:CHUNK, :]
    g_end = jnp.exp(gc_last)

    kk = _bmm_nt(k, k)
    lmat = jnp.where(strict[None], b3[:, :, :CHUNK] * kk * decay, 0.0)
    eye_c = (r == c).astype(f32)[None]
    pk = -lmat
    tinv = eye_c + pk
    for _ in range(5):
        pk = _bmm(pk, pk, HIGHEST)
        tinv = tinv + _bmm(tinv, pk, HIGHEST)
    rhs = jnp.concatenate([b3 * v, (b3 * gam) * k], axis=-1)
    sol = _bmm(tinv, rhs, HIGHEST)
    u_new = sol[:, :, :DN_HEAD_DIM]
    k_cum = sol[:, :, DN_HEAD_DIM:]

    aqk = _bmm_nt(q, k) * decay
    q_dec = q * gam
    k_dec = k * jnp.exp(gc_last - gc3)

    state = state_ref[...]
    w = u_new - _bmm(k_cum, state)
    o = _bmm(q_dec, state) + _bmm(aqk, w)
    state_ref[...] = g_end * state + _bmm(jnp.swapaxes(k_dec, 1, 2), w)

    o = o * lax.rsqrt(jnp.mean(o * o, axis=-1, keepdims=True) + NORM_EPS) * ng_ref[...]
    o2d = jnp.concatenate([o[h] for h in range(DN_HEADS)], axis=-1)
    o_ref[...] = (o2d * _silu(z_ref[...])).astype(o_ref.dtype)


def _deltanet(proj, ba, conv_w, alog_row, dtb_row, ng):
    s = proj.shape[0]
    qkv_blk = 3 * DN_WIDTH
    full2 = lambda shp: pl.BlockSpec(shp, lambda i: (0, 0))
    return pl.pallas_call(
        _deltanet_kernel,
        grid=(s // CHUNK,),
        in_specs=[pl.BlockSpec((CHUNK, qkv_blk), lambda i: (i, 3 * GMLP_WIDTH // qkv_blk)),
                  pl.BlockSpec((CHUNK, LANES), lambda i: (i, 0)),
                  pl.BlockSpec((CHUNK, DN_WIDTH), lambda i: (i, (3 * GMLP_WIDTH + qkv_blk) // DN_WIDTH)),
                  full2((CONV_WIDTH, qkv_blk)), full2((1, LANES)), full2((1, LANES)),
                  full2((1, DN_HEAD_DIM))],
        out_specs=pl.BlockSpec((CHUNK, DN_WIDTH), lambda i: (i, 0)),
        out_shape=jax.ShapeDtypeStruct((s, DN_WIDTH), jnp.bfloat16),
        scratch_shapes=[pltpu.VMEM((SUBLANES + CHUNK, qkv_blk), jnp.float32),
                        pltpu.VMEM((LANES, CHUNK), jnp.float32),
                        pltpu.VMEM((DN_HEADS, DN_HEAD_DIM, DN_HEAD_DIM), jnp.float32)],
        compiler_params=_params("arbitrary"),
        name="gated_deltanet",
    )(proj, ba, proj, conv_w, alog_row, dtb_row, ng.reshape(1, DN_HEAD_DIM))


def _merge_kernel(h_ref, agm_ref, adn_ref, wggm_ref, wgdn_ref, wbgm_ref, wbdn_ref, o_ref):
    f32 = jnp.float32
    h = h_ref[...]
    gate_gm = _sigmoid(jnp.dot(h, wggm_ref[...], preferred_element_type=f32))
    y_gm = jnp.dot(agm_ref[...], wbgm_ref[...], preferred_element_type=f32)
    acc = gate_gm * y_gm
    gate_dn = _sigmoid(jnp.dot(h, wgdn_ref[...], preferred_element_type=f32))
    y_dn = jnp.dot(adn_ref[...], wbdn_ref[...], preferred_element_type=f32)
    o_ref[...] = (acc + gate_dn * y_dn).astype(o_ref.dtype)


def _merge(h, a_gm, a_dn, w_gate, w_br_gm, w_br_dn, tm=512, tn=512):
    s, d = h.shape
    w = a_gm.shape[1]
    ngate = d // tn
    return pl.pallas_call(
        _merge_kernel,
        grid=(s // tm, d // tn),
        in_specs=[pl.BlockSpec((tm, d), lambda i, j: (i, 0)),
                  pl.BlockSpec((tm, w), lambda i, j: (i, 0)),
                  pl.BlockSpec((tm, w), lambda i, j: (i, 0)),
                  pl.BlockSpec((d, tn), lambda i, j: (0, j)),
                  pl.BlockSpec((d, tn), lambda i, j: (0, j + ngate)),
                  pl.BlockSpec((w, tn), lambda i, j: (0, j)),
                  pl.BlockSpec((w, tn), lambda i, j: (0, j))],
        out_specs=pl.BlockSpec((tm, tn), lambda i, j: (i, j)),
        out_shape=jax.ShapeDtypeStruct((s, d), jnp.bfloat16),
        compiler_params=_params("arbitrary", "arbitrary"),
        name="gated_merge",
    )(h, a_gm, a_dn, w_gate, w_gate, w_br_gm, w_br_dn)


def _outproj_kernel(m_ref, w_ref, x_ref, o_ref):
    o_ref[...] = x_ref[...] + jnp.dot(m_ref[...], w_ref[...], preferred_element_type=jnp.float32)


def _outproj(merged, w_out, x, tm=1024, tn=512):
    s, d = x.shape
    return pl.pallas_call(
        _outproj_kernel,
        grid=(s // tm, d // tn),
        in_specs=[pl.BlockSpec((tm, d), lambda i, j: (i, 0)),
                  pl.BlockSpec((d, tn), lambda i, j: (0, j)),
                  pl.BlockSpec((tm, tn), lambda i, j: (i, j))],
        out_specs=pl.BlockSpec((tm, tn), lambda i, j: (i, j)),
        out_shape=jax.ShapeDtypeStruct((s, d), jnp.float32),
        compiler_params=_params("arbitrary", "arbitrary"),
        name="out_proj_residual",
    )(merged, w_out, x)


def kernel(x, norm_g, w_in, conv_w, a_log, dt_bias, dn_norm_g, ln_g, ln_b, w_s, b_s,
           w_br_gmlp, w_br_dn, w_out, final_g):
    bsz, seq, d = x.shape
    bf16 = jnp.bfloat16
    xs = x.reshape(bsz * seq, d)
    head_pad = jnp.zeros((DN_HEADS,), jnp.float32)
    tail_pad = jnp.zeros((LANES - N_BA,), jnp.float32)
    for l in range(DEPTH):
        w_main = w_in[l][:, :N_MAIN].astype(bf16)
        w_ba = jnp.pad(w_in[l][:, N_MAIN:N_MAIN + N_BA], ((0, 0), (0, LANES - N_BA))).astype(bf16)
        w_gate = w_in[l][:, N_MAIN + N_BA:].astype(bf16)
        alog_row = jnp.concatenate([head_pad, a_log[l], tail_pad]).reshape(1, LANES)
        dtb_row = jnp.concatenate([head_pad, dt_bias[l], tail_pad]).reshape(1, LANES)
        bexp = jnp.repeat(b_s[l].T, GMLP_GROUP_DIM, axis=1)

        h = _rmsnorm(xs, norm_g[l], bf16)
        proj = _matmul(h, w_main, jnp.float32, 1024, 512, "in_proj_main")
        ba = _matmul(h, w_ba, jnp.float32, 1024, LANES, "in_proj_ba")
        a_gm = _gmlp(proj, ln_g[l], ln_b[l], w_s[l], bexp)
        a_dn = _deltanet(proj, ba, conv_w[l], alog_row, dtb_row, dn_norm_g[l])
        merged = _merge(h, a_gm, a_dn, w_gate, w_br_gmlp[l].astype(bf16), w_br_dn[l].astype(bf16))
        xs = _outproj(merged, w_out[l].astype(bf16), xs)
    out = _rmsnorm(xs, final_g, jnp.float32)
    return out.reshape(bsz, seq, d)
```

```python
import functools

import jax
import jax.numpy as jnp
from jax import lax
from jax.experimental import pallas as pl
from jax.experimental.pallas import tpu as pltpu

D_MODEL = 4096
DEPTH = 4
CHUNK = 64
GMLP_BLOCK = 128
GMLP_WIDTH = D_MODEL // 2
GMLP_GROUP_DIM = 128
GMLP_GROUPS = GMLP_WIDTH // GMLP_GROUP_DIM
DN_WIDTH = D_MODEL // 2
DN_HEAD_DIM = 128
DN_HEADS = DN_WIDTH // DN_HEAD_DIM
CONV_WIDTH = 4
NORM_EPS = 1e-6
LANES = 128
SUBLANES = 8
N_MAIN = 3 * GMLP_WIDTH + 4 * DN_WIDTH
N_BA = 2 * DN_HEADS
VMEM_LIMIT = 56 * 1024 * 1024

LOG2_E = 1.4426950408889634
DN_STEP_CHUNKS = 4


def _params(*sem):
    return pltpu.CompilerParams(dimension_semantics=sem, vmem_limit_bytes=VMEM_LIMIT)


def _sigmoid(x):
    return 1.0 / (1.0 + jnp.exp2(x * (-LOG2_E)))


def _silu(x):
    return x * _sigmoid(x)


def _gelu(x):
    return 0.5 * x * (1.0 + lax.erf(x * (2.0 ** -0.5)))


def _rmsnorm_kernel(x_ref, g_ref, o_ref):
    x = x_ref[...]
    y = x * lax.rsqrt(jnp.mean(x * x, axis=-1, keepdims=True) + NORM_EPS)
    o_ref[...] = (y * g_ref[...]).astype(o_ref.dtype)


def _rmsnorm(x, g, out_dtype, tr=512):
    s, d = x.shape
    return pl.pallas_call(
        _rmsnorm_kernel,
        grid=(s // tr,),
        in_specs=[pl.BlockSpec((tr, d), lambda i: (i, 0)),
                  pl.BlockSpec((1, d), lambda i: (0, 0))],
        out_specs=pl.BlockSpec((tr, d), lambda i: (i, 0)),
        out_shape=jax.ShapeDtypeStruct((s, d), out_dtype),
        compiler_params=_params("arbitrary"),
        name="rmsnorm",
    )(x, g.reshape(1, d))


def _proj_kernel(*refs, has_residual):
    if has_residual:
        a_ref, w_ref, x_ref, o_ref, wb_ref = refs
    else:
        a_ref, w_ref, o_ref, wb_ref = refs

    @pl.when(pl.program_id(1) == 0)
    def _():
        wb_ref[...] = w_ref[...].astype(wb_ref.dtype)

    acc = jnp.dot(a_ref[...], wb_ref[...], preferred_element_type=jnp.float32)
    if has_residual:
        acc = acc + x_ref[...]
    o_ref[...] = acc.astype(o_ref.dtype)


def _proj(a, w3, layer, col0, n, tm, tn, name, residual=None):
    m, k = a.shape
    assert col0 % tn == 0 and n % tn == 0 and m % tm == 0
    cb = col0 // tn
    in_specs = [pl.BlockSpec((tm, k), lambda j, i: (i, 0)),
                pl.BlockSpec((None, k, tn), lambda j, i: (layer, 0, j + cb))]
    args = [a, w3]
    if residual is not None:
        in_specs.append(pl.BlockSpec((tm, tn), lambda j, i: (i, j)))
        args.append(residual)
    return pl.pallas_call(
        functools.partial(_proj_kernel, has_residual=residual is not None),
        grid=(n // tn, m // tm),
        in_specs=in_specs,
        out_specs=pl.BlockSpec((tm, tn), lambda j, i: (i, j)),
        out_shape=jax.ShapeDtypeStruct((m, n), jnp.float32),
        scratch_shapes=[pltpu.VMEM((k, tn), jnp.bfloat16)],
        compiler_params=_params("arbitrary", "arbitrary"),
        name=name,
    )(*args)


def _gmlp_kernel(u_ref, v_ref, z_ref, lng_ref, lnb_ref, ws_ref, bexp_ref, o_ref):
    u = _gelu(u_ref[...])
    v = _gelu(v_ref[...])
    mu = jnp.mean(v, axis=-1, keepdims=True)
    vc = v - mu
    vn = vc * lax.rsqrt(jnp.mean(vc * vc, axis=-1, keepdims=True) + NORM_EPS)
    vn = vn * lng_ref[...] + lnb_ref[...]
    row = lax.broadcasted_iota(jnp.int32, (GMLP_BLOCK, GMLP_BLOCK), 0) // CHUNK
    col = lax.broadcasted_iota(jnp.int32, (GMLP_BLOCK, GMLP_BLOCK), 1) // CHUNK
    mask = col <= row
    gate = _silu(z_ref[...])
    for g in range(GMLP_GROUPS):
        sl = slice(g * GMLP_GROUP_DIM, (g + 1) * GMLP_GROUP_DIM)
        ws = jnp.where(mask, ws_ref[g], 0.0)
        mixed = jnp.dot(ws, vn[:, sl], preferred_element_type=jnp.float32) + bexp_ref[:, sl]
        o_ref[:, sl] = (u[:, sl] * mixed * gate[:, sl]).astype(o_ref.dtype)


def _gmlp(proj, ln_g, ln_b, w_s, bexp):
    s = proj.shape[0]
    w = GMLP_WIDTH
    row = lambda c: pl.BlockSpec((GMLP_BLOCK, w), lambda i, c=c: (i, c))
    full2 = lambda shp: pl.BlockSpec(shp, lambda i: (0, 0))
    return pl.pallas_call(
        _gmlp_kernel,
        grid=(s // GMLP_BLOCK,),
        in_specs=[row(0), row(1), row(2), full2((1, w)), full2((1, w)),
                  pl.BlockSpec((GMLP_GROUPS, GMLP_BLOCK, GMLP_BLOCK), lambda i: (0, 0, 0)),
                  full2((GMLP_BLOCK, w))],
        out_specs=pl.BlockSpec((GMLP_BLOCK, w), lambda i: (i, 0)),
        out_shape=jax.ShapeDtypeStruct((s, w), jnp.bfloat16),
        compiler_params=_params("arbitrary"),
        name="gmlp_gating",
    )(proj, proj, proj, ln_g.reshape(1, w), ln_b.reshape(1, w), w_s, bexp)


def _bmm(a, b):
    return jnp.einsum("hij,hjk->hik", a, b, preferred_element_type=jnp.float32)


def _bmm_nt(a, b):
    return jnp.einsum("hid,hjd->hij", a, b, preferred_element_type=jnp.float32)


def _split3(x):
    bf16, f32 = jnp.bfloat16, jnp.float32
    hi = x.astype(bf16)
    r1 = x - hi.astype(f32)
    mid = r1.astype(bf16)
    lo = (r1 - mid.astype(f32)).astype(bf16)
    return hi, mid, lo


def _deltanet_kernel(qkv_ref, ba_ref, z_ref, convw_ref, alog_ref, dtb_ref, ng_ref, o_ref,
                     cbuf, rows_ref, state_ref):
    f32, bf16 = jnp.float32, jnp.bfloat16
    i = pl.program_id(0)
    halo = SUBLANES
    nh = DN_HEADS
    nc = DN_STEP_CHUNKS
    rows = nc * CHUNK
    nb = nc * nh

    @pl.when(i == 0)
    def _():
        cbuf[0:halo, :] = jnp.zeros((halo, 3 * DN_WIDTH), f32)
        state_ref[...] = jnp.zeros(state_ref.shape, f32)

    cbuf[halo:halo + rows, :] = qkv_ref[...]
    conv = convw_ref[CONV_WIDTH - 1:CONV_WIDTH, :] * cbuf[halo:halo + rows, :]
    for t in range(CONV_WIDTH - 1):
        off = halo - (CONV_WIDTH - 1) + t
        conv = conv + convw_ref[t:t + 1, :] * cbuf[off:off + rows, :]
    cbuf[0:halo, :] = cbuf[rows:rows + halo, :]
    qkv = _silu(conv)

    def split(x2d):
        return jnp.stack([x2d[ci * CHUNK:(ci + 1) * CHUNK, h * DN_HEAD_DIM:(h + 1) * DN_HEAD_DIM]
                          for ci in range(nc) for h in range(nh)], axis=0)

    q = split(qkv[:, :DN_WIDTH])
    k = split(qkv[:, DN_WIDTH:2 * DN_WIDTH])
    v = split(qkv[:, 2 * DN_WIDTH:])
    ones = jnp.ones((DN_HEAD_DIM, DN_HEAD_DIM), f32)

    def sumsq(x):
        sq = (x * x).reshape(nb * CHUNK, DN_HEAD_DIM)
        return jnp.dot(sq, ones, preferred_element_type=f32).reshape(nb, CHUNK, DN_HEAD_DIM)

    q = q * lax.rsqrt((sumsq(q) + NORM_EPS) * float(DN_HEAD_DIM))
    k = k * lax.rsqrt(sumsq(k) + NORM_EPS)

    ba = ba_ref[...]
    beta_all = _sigmoid(ba)
    xg = ba + dtb_ref[...]
    softplus = jnp.maximum(xg, 0.0) + jnp.log1p(jnp.exp(-jnp.abs(xg)))
    g_all = -jnp.exp(alog_ref[...]) * softplus
    r = lax.broadcasted_iota(jnp.int32, (CHUNK, CHUNK), 0)
    c = lax.broadcasted_iota(jnp.int32, (CHUNK, CHUNK), 1)
    incl = r >= c
    strict = r > c
    rr = lax.broadcasted_iota(jnp.int32, (rows, rows), 0)
    cc = lax.broadcasted_iota(jnp.int32, (rows, rows), 1)
    tril = ((rr >= cc) & ((rr // CHUNK) == (cc // CHUNK))).astype(bf16)
    gcum_all = jnp.dot(jnp.concatenate([tril, tril, tril], axis=1),
                       jnp.concatenate(_split3(g_all), axis=0), preferred_element_type=f32)

    lane = lax.broadcasted_iota(jnp.int32, (rows, LANES), 1)
    merged = jnp.where(lane < nh, beta_all, gcum_all)
    for ci in range(nc):
        rows_ref[ci] = merged[ci * CHUNK:(ci + 1) * CHUNK, :].T
    pairs = [(ci, h) for ci in range(nc) for h in range(nh)]
    b_row = jnp.stack([rows_ref[ci, h:h + 1, :] for ci, h in pairs], axis=0)
    gc_row = jnp.stack([rows_ref[ci, nh + h:nh + h + 1, :] for ci, h in pairs], axis=0)
    b_col = jnp.stack([jnp.broadcast_to(beta_all[ci * CHUNK:(ci + 1) * CHUNK, h:h + 1], (CHUNK, LANES))
                       for ci, h in pairs], axis=0)
    gc_col = jnp.stack([jnp.broadcast_to(gcum_all[ci * CHUNK:(ci + 1) * CHUNK, nh + h:nh + h + 1], (CHUNK, LANES))
                        for ci, h in pairs], axis=0)

    gam = jnp.exp(gc_col)
    gc_last = gc_col[:, CHUNK - 1:CHUNK, :]
    g_end = jnp.exp(gc_last)
    kdb = jnp.exp(gc_last - gc_col) * b_col
    decb = jnp.exp(gc_col[:, :, :CHUNK] - gc_row) * b_row

    a = jnp.where(strict[None], _bmm_nt(k, k) * decb, 0.0)
    eye_c = (r == c).astype(f32)[None]
    same_blk = ((r // SUBLANES) == (c // SUBLANES))[None]

    def inv_nilpotent8(p):
        pb = (-p).astype(bf16)
        t = eye_c - p
        for _ in range(2):
            pb = _bmm(pb, pb).astype(bf16)
            t = t + _bmm(t.astype(bf16), pb)
        return t

    t_d = inv_nilpotent8(jnp.where(same_blk, a, 0.0))
    t_db = t_d.astype(bf16)
    t_n = inv_nilpotent8(_bmm(t_db, jnp.where(same_blk, 0.0, a).astype(bf16)))
    tinv = _bmm(t_n.astype(bf16), t_db)
    rhs = jnp.concatenate([v, gam * k], axis=-1)
    sol = _bmm(tinv, rhs)
    u_s = sol[:, :, :DN_HEAD_DIM]
    kc_s = sol[:, :, DN_HEAD_DIM:]

    aqkb = jnp.where(incl[None], _bmm_nt(q, k) * decb, 0.0)
    lhs = jnp.concatenate([kc_s, q * gam], axis=1)
    kd = k * kdb

    state = state_ref[...]
    outs = []
    for ci in range(nc):
        sl = slice(ci * nh, (ci + 1) * nh)
        prod = _bmm(lhs[sl], state)
        w_s = u_s[sl] - prod[:, :CHUNK]
        outs.append(prod[:, CHUNK:] + _bmm(aqkb[sl], w_s))
        state = g_end[sl] * state + _bmm(jnp.swapaxes(kd[sl], 1, 2), w_s)
    state_ref[...] = state

    o = jnp.concatenate(outs, axis=0)
    o = o * lax.rsqrt(sumsq(o) * (1.0 / DN_HEAD_DIM) + NORM_EPS) * ng_ref[...]
    o2d = jnp.concatenate([jnp.concatenate([o[ci * nh + h] for h in range(nh)], axis=-1)
                           for ci in range(nc)], axis=0)
    o_ref[...] = (o2d * _silu(z_ref[...])).astype(o_ref.dtype)


def _deltanet(proj, ba, conv_w, alog_row, dtb_row, ng):
    s = proj.shape[0]
    rows = DN_STEP_CHUNKS * CHUNK
    assert s % rows == 0
    qkv_blk = 3 * DN_WIDTH
    full2 = lambda shp: pl.BlockSpec(shp, lambda i: (0, 0))
    return pl.pallas_call(
        _deltanet_kernel,
        grid=(s // rows,),
        in_specs=[pl.BlockSpec((rows, qkv_blk), lambda i: (i, 3 * GMLP_WIDTH // qkv_blk)),
                  pl.BlockSpec((rows, LANES), lambda i: (i, 0)),
                  pl.BlockSpec((rows, DN_WIDTH), lambda i: (i, (3 * GMLP_WIDTH + qkv_blk) // DN_WIDTH)),
                  full2((CONV_WIDTH, qkv_blk)), full2((1, LANES)), full2((1, LANES)),
                  full2((1, DN_HEAD_DIM))],
        out_specs=pl.BlockSpec((rows, DN_WIDTH), lambda i: (i, 0)),
        out_shape=jax.ShapeDtypeStruct((s, DN_WIDTH), jnp.bfloat16),
        scratch_shapes=[pltpu.VMEM((SUBLANES + rows, qkv_blk), jnp.float32),
                        pltpu.VMEM((DN_STEP_CHUNKS, LANES, CHUNK), jnp.float32),
                        pltpu.VMEM((DN_HEADS, DN_HEAD_DIM, DN_HEAD_DIM), jnp.float32)],
        compiler_params=_params("arbitrary"),
        name="gated_deltanet",
    )(proj, ba, proj, conv_w, alog_row, dtb_row, ng.reshape(1, DN_HEAD_DIM))


def _merge_kernel(h_ref, agm_ref, adn_ref, wggm_ref, wgdn_ref, wbgm_ref, wbdn_ref, o_ref):
    f32 = jnp.float32
    h = h_ref[...]
    gate_gm = _sigmoid(jnp.dot(h, wggm_ref[...], preferred_element_type=f32))
    y_gm = jnp.dot(agm_ref[...], wbgm_ref[...], preferred_element_type=f32)
    acc = gate_gm * y_gm
    gate_dn = _sigmoid(jnp.dot(h, wgdn_ref[...], preferred_element_type=f32))
    y_dn = jnp.dot(adn_ref[...], wbdn_ref[...], preferred_element_type=f32)
    o_ref[...] = (acc + gate_dn * y_dn).astype(o_ref.dtype)


def _merge(h, a_gm, a_dn, w_gate, w_br_gm, w_br_dn, tm=512, tn=512):
    s, d = h.shape
    w = a_gm.shape[1]
    ngate = d // tn
    return pl.pallas_call(
        _merge_kernel,
        grid=(s // tm, d // tn),
        in_specs=[pl.BlockSpec((tm, d), lambda i, j: (i, 0)),
                  pl.BlockSpec((tm, w), lambda i, j: (i, 0)),
                  pl.BlockSpec((tm, w), lambda i, j: (i, 0)),
                  pl.BlockSpec((d, tn), lambda i, j: (0, j)),
                  pl.BlockSpec((d, tn), lambda i, j: (0, j + ngate)),
                  pl.BlockSpec((w, tn), lambda i, j: (0, j)),
                  pl.BlockSpec((w, tn), lambda i, j: (0, j))],
        out_specs=pl.BlockSpec((tm, tn), lambda i, j: (i, j)),
        out_shape=jax.ShapeDtypeStruct((s, d), jnp.bfloat16),
        compiler_params=_params("arbitrary", "arbitrary"),
        name="gated_merge",
    )(h, a_gm, a_dn, w_gate, w_gate, w_br_gm, w_br_dn)


def kernel(x, norm_g, w_in, conv_w, a_log, dt_bias, dn_norm_g, ln_g, ln_b, w_s, b_s,
           w_br_gmlp, w_br_dn, w_out, final_g):
    bsz, seq, d = x.shape
    assert bsz == 1, "the conv halo and the delta-rule state are carried across row tiles of one sequence"
    bf16 = jnp.bfloat16
    xs = x.reshape(bsz * seq, d)
    head_pad = jnp.zeros((DN_HEADS,), jnp.float32)
    tail_pad = jnp.zeros((LANES - N_BA,), jnp.float32)
    for l in range(DEPTH):
        w_gate = w_in[l][:, N_MAIN + N_BA:].astype(bf16)
        alog_row = jnp.concatenate([head_pad, a_log[l], tail_pad]).reshape(1, LANES)
        dtb_row = jnp.concatenate([head_pad, dt_bias[l], tail_pad]).reshape(1, LANES)
        bexp = jnp.repeat(b_s[l].T, GMLP_GROUP_DIM, axis=1)

        h = _rmsnorm(xs, norm_g[l], bf16)
        proj = _proj(h, w_in, l, 0, N_MAIN, 1024, 512, "in_proj_main")
        ba = _proj(h, w_in, l, N_MAIN, LANES, 1024, LANES, "in_proj_ba")
        a_gm = _gmlp(proj, ln_g[l], ln_b[l], w_s[l], bexp)
        a_dn = _deltanet(proj, ba, conv_w[l], alog_row, dtb_row, dn_norm_g[l])
        merged = _merge(h, a_gm, a_dn, w_gate, w_br_gmlp[l].astype(bf16), w_br_dn[l].astype(bf16))
        xs = _proj(merged, w_out, l, 0, d, 1024, 512, "out_proj_residual", residual=xs)
    out = _rmsnorm(xs, final_g, jnp.float32)
    return out.reshape(bsz, seq, d)
```

```python
import functools

import jax
import jax.numpy as jnp
from jax import lax
from jax.experimental import pallas as pl
from jax.experimental.pallas import tpu as pltpu

D_MODEL = 4096
DEPTH = 4
CHUNK = 64
GMLP_BLOCK = 128
GMLP_WIDTH = D_MODEL // 2
GMLP_GROUP_DIM = 128
GMLP_GROUPS = GMLP_WIDTH // GMLP_GROUP_DIM
DN_WIDTH = D_MODEL // 2
DN_HEAD_DIM = 128
DN_HEADS = DN_WIDTH // DN_HEAD_DIM
CONV_WIDTH = 4
NORM_EPS = 1e-6
LANES = 128
SUBLANES = 8
N_MAIN = 3 * GMLP_WIDTH + 4 * DN_WIDTH
N_BA = 2 * DN_HEADS
VMEM_LIMIT = 56 * 1024 * 1024

LOG2_E = 1.4426950408889634
DN_STEP_CHUNKS = 4
GMLP_STEP_BLOCKS = 4


def _params(*sem):
    return pltpu.CompilerParams(dimension_semantics=sem, vmem_limit_bytes=VMEM_LIMIT)


def _sigmoid(x):
    return 1.0 / (1.0 + jnp.exp2(x * (-LOG2_E)))


def _silu(x):
    return x * _sigmoid(x)


def _gelu(x):
    return 0.5 * x * (1.0 + lax.erf(x * (2.0 ** -0.5)))


def _rmsnorm_kernel(x_ref, g_ref, o_ref):
    x = x_ref[...]
    y = x * lax.rsqrt(jnp.mean(x * x, axis=-1, keepdims=True) + NORM_EPS)
    o_ref[...] = (y * g_ref[...]).astype(o_ref.dtype)


def _rmsnorm(x, g, out_dtype, tr=512):
    s, d = x.shape
    return pl.pallas_call(
        _rmsnorm_kernel,
        grid=(s // tr,),
        in_specs=[pl.BlockSpec((tr, d), lambda i: (i, 0)),
                  pl.BlockSpec((1, d), lambda i: (0, 0))],
        out_specs=pl.BlockSpec((tr, d), lambda i: (i, 0)),
        out_shape=jax.ShapeDtypeStruct((s, d), out_dtype),
        compiler_params=_params("arbitrary"),
        name="rmsnorm",
    )(x, g.reshape(1, d))


def _proj_kernel(*refs, has_residual, w_is_nk):
    if has_residual:
        a_ref, w_ref, x_ref, o_ref, wb_ref = refs
    else:
        a_ref, w_ref, o_ref, wb_ref = refs

    @pl.when(pl.program_id(1) == 0)
    def _():
        w = w_ref[...]
        wb_ref[...] = (w.T if w_is_nk else w).astype(wb_ref.dtype)

    acc = jnp.dot(a_ref[...], wb_ref[...], preferred_element_type=jnp.float32)
    if has_residual:
        acc = acc + x_ref[...]
    o_ref[...] = acc.astype(o_ref.dtype)


def _proj(a, w3, layer, col0, n, tm, tn, name, residual=None, w_is_nk=False):
    m, k = a.shape
    assert col0 % tn == 0 and n % tn == 0 and m % tm == 0
    cb = col0 // tn
    if w_is_nk:
        w_spec = pl.BlockSpec((None, tn, k), lambda j, i: (layer, j + cb, 0))
    else:
        w_spec = pl.BlockSpec((None, k, tn), lambda j, i: (layer, 0, j + cb))
    in_specs = [pl.BlockSpec((tm, k), lambda j, i: (i, 0)), w_spec]
    args = [a, w3]
    if residual is not None:
        in_specs.append(pl.BlockSpec((tm, tn), lambda j, i: (i, j)))
        args.append(residual)
    return pl.pallas_call(
        functools.partial(_proj_kernel, has_residual=residual is not None, w_is_nk=w_is_nk),
        grid=(n // tn, m // tm),
        in_specs=in_specs,
        out_specs=pl.BlockSpec((tm, tn), lambda j, i: (i, j)),
        out_shape=jax.ShapeDtypeStruct((m, n), jnp.float32),
        scratch_shapes=[pltpu.VMEM((k, tn), jnp.bfloat16)],
        compiler_params=_params("arbitrary", "arbitrary"),
        name=name,
    )(*args)


def _cast_kernel(w_ref, o_ref):
    o_ref[...] = w_ref[0].astype(o_ref.dtype)


def _cast_rows(w3, layer, row0, nrows, name, tr=512):
    k = w3.shape[2]
    assert row0 % (2 * SUBLANES) == 0 and nrows % tr == 0
    return pl.pallas_call(
        _cast_kernel,
        grid=(nrows // tr,),
        in_specs=[pl.BlockSpec((pl.Element(1), pl.Element(tr), pl.Element(k)),
                               lambda i: (layer, pl.multiple_of(row0 + i * tr, 2 * SUBLANES), 0))],
        out_specs=pl.BlockSpec((tr, k), lambda i: (i, 0)),
        out_shape=jax.ShapeDtypeStruct((nrows, k), jnp.bfloat16),
        compiler_params=_params("arbitrary"),
        name=name,
    )(w3)


def _gmlp_kernel(u_ref, v_ref, z_ref, lng_ref, lnb_ref, ws_ref, bexp_ref, o_ref):
    row = lax.broadcasted_iota(jnp.int32, (GMLP_BLOCK, GMLP_BLOCK), 0) // CHUNK
    col = lax.broadcasted_iota(jnp.int32, (GMLP_BLOCK, GMLP_BLOCK), 1) // CHUNK
    mask = col <= row
    for blk in range(GMLP_STEP_BLOCKS):
        rs = slice(blk * GMLP_BLOCK, (blk + 1) * GMLP_BLOCK)
        u = _gelu(u_ref[rs, :])
        v = _gelu(v_ref[rs, :])
        mu = jnp.mean(v, axis=-1, keepdims=True)
        vc = v - mu
        vn = vc * lax.rsqrt(jnp.mean(vc * vc, axis=-1, keepdims=True) + NORM_EPS)
        vn = vn * lng_ref[...] + lnb_ref[...]
        gate = _silu(z_ref[rs, :])
        for g in range(GMLP_GROUPS):
            sl = slice(g * GMLP_GROUP_DIM, (g + 1) * GMLP_GROUP_DIM)
            ws = jnp.where(mask, ws_ref[g], 0.0)
            mixed = jnp.dot(ws, vn[:, sl], preferred_element_type=jnp.float32) + bexp_ref[:, sl]
            o_ref[rs, sl] = (u[:, sl] * mixed * gate[:, sl]).astype(o_ref.dtype)


def _gmlp(proj, ln_g, ln_b, w_s, bexp):
    s = proj.shape[0]
    w = GMLP_WIDTH
    tr = GMLP_STEP_BLOCKS * GMLP_BLOCK
    assert s % tr == 0
    row = lambda c: pl.BlockSpec((tr, w), lambda i, c=c: (i, c))
    full2 = lambda shp: pl.BlockSpec(shp, lambda i: (0, 0))
    return pl.pallas_call(
        _gmlp_kernel,
        grid=(s // tr,),
        in_specs=[row(0), row(1), row(2), full2((1, w)), full2((1, w)),
                  pl.BlockSpec((GMLP_GROUPS, GMLP_BLOCK, GMLP_BLOCK), lambda i: (0, 0, 0)),
                  full2((GMLP_BLOCK, w))],
        out_specs=pl.BlockSpec((tr, w), lambda i: (i, 0)),
        out_shape=jax.ShapeDtypeStruct((s, w), jnp.bfloat16),
        compiler_params=_params("arbitrary"),
        name="gmlp_gating",
    )(proj, proj, proj, ln_g.reshape(1, w), ln_b.reshape(1, w), w_s, bexp)


def _bmm(a, b):
    return jnp.einsum("hij,hjk->hik", a, b, preferred_element_type=jnp.float32)


def _bmm_nt(a, b):
    return jnp.einsum("hid,hjd->hij", a, b, preferred_element_type=jnp.float32)


def _split3(x):
    bf16, f32 = jnp.bfloat16, jnp.float32
    hi = x.astype(bf16)
    r1 = x - hi.astype(f32)
    mid = r1.astype(bf16)
    lo = (r1 - mid.astype(f32)).astype(bf16)
    return hi, mid, lo


def _deltanet_kernel(qkv_ref, ba_ref, z_ref, convw_ref, alog_ref, dtb_ref, ng_ref, o_ref,
                     cbuf, rows_ref, state_ref):
    f32, bf16 = jnp.float32, jnp.bfloat16
    i = pl.program_id(0)
    halo = SUBLANES
    nh = DN_HEADS
    nc = DN_STEP_CHUNKS
    rows = nc * CHUNK
    nb = nc * nh

    @pl.when(i == 0)
    def _():
        cbuf[0:halo, :] = jnp.zeros((halo, 3 * DN_WIDTH), f32)
        state_ref[...] = jnp.zeros(state_ref.shape, f32)

    cbuf[halo:halo + rows, :] = qkv_ref[...]
    conv = convw_ref[CONV_WIDTH - 1:CONV_WIDTH, :] * cbuf[halo:halo + rows, :]
    for t in range(CONV_WIDTH - 1):
        off = halo - (CONV_WIDTH - 1) + t
        conv = conv + convw_ref[t:t + 1, :] * cbuf[off:off + rows, :]
    cbuf[0:halo, :] = cbuf[rows:rows + halo, :]
    qkv = _silu(conv)

    def split(x2d):
        return jnp.stack([x2d[ci * CHUNK:(ci + 1) * CHUNK, h * DN_HEAD_DIM:(h + 1) * DN_HEAD_DIM]
                          for ci in range(nc) for h in range(nh)], axis=0)

    q = split(qkv[:, :DN_WIDTH])
    k = split(qkv[:, DN_WIDTH:2 * DN_WIDTH])
    v = split(qkv[:, 2 * DN_WIDTH:])
    ones = jnp.ones((DN_HEAD_DIM, DN_HEAD_DIM), f32)

    def sumsq(x):
        sq = (x * x).reshape(nb * CHUNK, DN_HEAD_DIM)
        return jnp.dot(sq, ones, preferred_element_type=f32).reshape(nb, CHUNK, DN_HEAD_DIM)

    q = q * lax.rsqrt((sumsq(q) + NORM_EPS) * float(DN_HEAD_DIM))
    k = k * lax.rsqrt(sumsq(k) + NORM_EPS)

    ba = ba_ref[...]
    beta_all = _sigmoid(ba)
    xg = ba + dtb_ref[...]
    softplus = jnp.maximum(xg, 0.0) + jnp.log1p(jnp.exp(-jnp.abs(xg)))
    g_all = -jnp.exp(alog_ref[...]) * softplus
    r = lax.broadcasted_iota(jnp.int32, (CHUNK, CHUNK), 0)
    c = lax.broadcasted_iota(jnp.int32, (CHUNK, CHUNK), 1)
    incl = r >= c
    strict = r > c
    rr = lax.broadcasted_iota(jnp.int32, (rows, rows), 0)
    cc = lax.broadcasted_iota(jnp.int32, (rows, rows), 1)
    tril = ((rr >= cc) & ((rr // CHUNK) == (cc // CHUNK))).astype(bf16)
    gcum_all = jnp.dot(jnp.concatenate([tril, tril, tril], axis=1),
                       jnp.concatenate(_split3(g_all), axis=0), preferred_element_type=f32)

    lane = lax.broadcasted_iota(jnp.int32, (rows, LANES), 1)
    merged = jnp.where(lane < nh, beta_all, gcum_all)
    for ci in range(nc):
        rows_ref[ci] = merged[ci * CHUNK:(ci + 1) * CHUNK, :].T
    pairs = [(ci, h) for ci in range(nc) for h in range(nh)]
    b_row = jnp.stack([rows_ref[ci, h:h + 1, :] for ci, h in pairs], axis=0)
    gc_row = jnp.stack([rows_ref[ci, nh + h:nh + h + 1, :] for ci, h in pairs], axis=0)
    b_col = jnp.stack([jnp.broadcast_to(beta_all[ci * CHUNK:(ci + 1) * CHUNK, h:h + 1], (CHUNK, LANES))
                       for ci, h in pairs], axis=0)
    gc_col = jnp.stack([jnp.broadcast_to(gcum_all[ci * CHUNK:(ci + 1) * CHUNK, nh + h:nh + h + 1], (CHUNK, LANES))
                        for ci, h in pairs], axis=0)

    gam = jnp.exp(gc_col)
    gc_last = gc_col[:, CHUNK - 1:CHUNK, :]
    g_end = jnp.exp(gc_last)
    kdb = jnp.exp(gc_last - gc_col) * b_col
    decb = jnp.exp(gc_col[:, :, :CHUNK] - gc_row) * b_row

    a = jnp.where(strict[None], _bmm_nt(k, k) * decb, 0.0)
    eye_c = (r == c).astype(f32)[None]
    same_blk = ((r // SUBLANES) == (c // SUBLANES))[None]

    def inv_nilpotent8(p):
        pb = (-p).astype(bf16)
        t = eye_c - p
        for _ in range(2):
            pb = _bmm(pb, pb).astype(bf16)
            t = t + _bmm(t.astype(bf16), pb)
        return t

    t_d = inv_nilpotent8(jnp.where(same_blk, a, 0.0))
    t_db = t_d.astype(bf16)
    t_n = inv_nilpotent8(_bmm(t_db, jnp.where(same_blk, 0.0, a).astype(bf16)))
    tinv = _bmm(t_n.astype(bf16), t_db)
    rhs = jnp.concatenate([v, gam * k], axis=-1)
    sol = _bmm(tinv, rhs)
    u_s = sol[:, :, :DN_HEAD_DIM]
    kc_s = sol[:, :, DN_HEAD_DIM:]

    aqkb = jnp.where(incl[None], _bmm_nt(q, k) * decb, 0.0)
    lhs = jnp.concatenate([kc_s, q * gam], axis=1)
    kd = k * kdb

    state = state_ref[...]
    outs = []
    for ci in range(nc):
        sl = slice(ci * nh, (ci + 1) * nh)
        prod = _bmm(lhs[sl], state)
        w_s = u_s[sl] - prod[:, :CHUNK]
        outs.append(prod[:, CHUNK:] + _bmm(aqkb[sl], w_s))
        state = g_end[sl] * state + _bmm(jnp.swapaxes(kd[sl], 1, 2), w_s)
    state_ref[...] = state

    o = jnp.concatenate(outs, axis=0)
    o = o * lax.rsqrt(sumsq(o) * (1.0 / DN_HEAD_DIM) + NORM_EPS) * ng_ref[...]
    o2d = jnp.concatenate([jnp.concatenate([o[ci * nh + h] for h in range(nh)], axis=-1)
                           for ci in range(nc)], axis=0)
    o_ref[...] = (o2d * _silu(z_ref[...])).astype(o_ref.dtype)


def _deltanet(proj, ba, conv_w, alog_row, dtb_row, ng):
    s = proj.shape[0]
    rows = DN_STEP_CHUNKS * CHUNK
    assert s % rows == 0
    qkv_blk = 3 * DN_WIDTH
    full2 = lambda shp: pl.BlockSpec(shp, lambda i: (0, 0))
    return pl.pallas_call(
        _deltanet_kernel,
        grid=(s // rows,),
        in_specs=[pl.BlockSpec((rows, qkv_blk), lambda i: (i, 3 * GMLP_WIDTH // qkv_blk)),
                  pl.BlockSpec((rows, LANES), lambda i: (i, 0)),
                  pl.BlockSpec((rows, DN_WIDTH), lambda i: (i, (3 * GMLP_WIDTH + qkv_blk) // DN_WIDTH)),
                  full2((CONV_WIDTH, qkv_blk)), full2((1, LANES)), full2((1, LANES)),
                  full2((1, DN_HEAD_DIM))],
        out_specs=pl.BlockSpec((rows, DN_WIDTH), lambda i: (i, 0)),
        out_shape=jax.ShapeDtypeStruct((s, DN_WIDTH), jnp.bfloat16),
        scratch_shapes=[pltpu.VMEM((SUBLANES + rows, qkv_blk), jnp.float32),
                        pltpu.VMEM((DN_STEP_CHUNKS, LANES, CHUNK), jnp.float32),
                        pltpu.VMEM((DN_HEADS, DN_HEAD_DIM, DN_HEAD_DIM), jnp.float32)],
        compiler_params=_params("arbitrary"),
        name="gated_deltanet",
    )(proj, ba, proj, conv_w, alog_row, dtb_row, ng.reshape(1, DN_HEAD_DIM))


def _merge_kernel(h_ref, agm_ref, adn_ref, wggm_ref, wgdn_ref, wbgm_ref, wbdn_ref, o_ref):
    f32 = jnp.float32
    h = h_ref[...]
    nt = (((1,), (1,)), ((), ()))
    gate_gm = _sigmoid(lax.dot_general(h, wggm_ref[...], nt, preferred_element_type=f32))
    y_gm = jnp.dot(agm_ref[...], wbgm_ref[...], preferred_element_type=f32)
    acc = gate_gm * y_gm
    gate_dn = _sigmoid(lax.dot_general(h, wgdn_ref[...], nt, preferred_element_type=f32))
    y_dn = jnp.dot(adn_ref[...], wbdn_ref[...], preferred_element_type=f32)
    o_ref[...] = (acc + gate_dn * y_dn).astype(o_ref.dtype)


def _merge(h, a_gm, a_dn, w_gate_t, w_br_gm, w_br_dn, tm=512, tn=512):
    s, d = h.shape
    w = a_gm.shape[1]
    ngate = d // tn
    return pl.pallas_call(
        _merge_kernel,
        grid=(s // tm, d // tn),
        in_specs=[pl.BlockSpec((tm, d), lambda i, j: (i, 0)),
                  pl.BlockSpec((tm, w), lambda i, j: (i, 0)),
                  pl.BlockSpec((tm, w), lambda i, j: (i, 0)),
                  pl.BlockSpec((tn, d), lambda i, j: (j, 0)),
                  pl.BlockSpec((tn, d), lambda i, j: (j + ngate, 0)),
                  pl.BlockSpec((w, tn), lambda i, j: (0, j)),
                  pl.BlockSpec((w, tn), lambda i, j: (0, j))],
        out_specs=pl.BlockSpec((tm, tn), lambda i, j: (i, j)),
        out_shape=jax.ShapeDtypeStruct((s, d), jnp.bfloat16),
        compiler_params=_params("arbitrary", "arbitrary"),
        name="gated_merge",
    )(h, a_gm, a_dn, w_gate_t, w_gate_t, w_br_gm, w_br_dn)


def kernel(x, norm_g, w_in, conv_w, a_log, dt_bias, dn_norm_g, ln_g, ln_b, w_s, b_s,
           w_br_gmlp, w_br_dn, w_out, final_g):
    bsz, seq, d = x.shape
    assert bsz == 1, "the conv halo and the delta-rule state are carried across row tiles of one sequence"
    bf16 = jnp.bfloat16
    xs = x.reshape(bsz * seq, d)
    head_pad = jnp.zeros((DN_HEADS,), jnp.float32)
    tail_pad = jnp.zeros((LANES - N_BA,), jnp.float32)
    w_in_t = jnp.swapaxes(w_in, 1, 2)
    for l in range(DEPTH):
        w_gate_t = _cast_rows(w_in_t, l, N_MAIN + N_BA, 2 * d, "gate_weight_cast")
        alog_row = jnp.concatenate([head_pad, a_log[l], tail_pad]).reshape(1, LANES)
        dtb_row = jnp.concatenate([head_pad, dt_bias[l], tail_pad]).reshape(1, LANES)
        bexp = jnp.repeat(b_s[l].T, GMLP_GROUP_DIM, axis=1)

        h = _rmsnorm(xs, norm_g[l], bf16)
        proj = _proj(h, w_in_t, l, 0, N_MAIN, 1024, 512, "in_proj_main", w_is_nk=True)
        ba = _proj(h, w_in_t, l, N_MAIN, LANES, 1024, LANES, "in_proj_ba", w_is_nk=True)
        a_gm = _gmlp(proj, ln_g[l], ln_b[l], w_s[l], bexp)
        a_dn = _deltanet(proj, ba, conv_w[l], alog_row, dtb_row, dn_norm_g[l])
        merged = _merge(h, a_gm, a_dn, w_gate_t, w_br_gmlp[l].astype(bf16), w_br_dn[l].astype(bf16))
        xs = _proj(merged, w_out, l, 0, d, 1024, 512, "out_proj_residual", residual=xs)
    out = _rmsnorm(xs, final_g, jnp.float32)
    return out.reshape(bsz, seq, d)
```

```python
import functools

import jax
import jax.numpy as jnp
from jax import lax
from jax.experimental import pallas as pl
from jax.experimental.pallas import tpu as pltpu

D_MODEL = 4096
DEPTH = 4
CHUNK = 64
GMLP_BLOCK = 128
GMLP_WIDTH = D_MODEL // 2
GMLP_GROUP_DIM = 128
GMLP_GROUPS = GMLP_WIDTH // GMLP_GROUP_DIM
DN_WIDTH = D_MODEL // 2
DN_HEAD_DIM = 128
DN_HEADS = DN_WIDTH // DN_HEAD_DIM
CONV_WIDTH = 4
NORM_EPS = 1e-6
LANES = 128
SUBLANES = 8
N_MAIN = 3 * GMLP_WIDTH + 4 * DN_WIDTH
N_BA = 2 * DN_HEADS
VMEM_LIMIT = 56 * 1024 * 1024

LOG2_E = 1.4426950408889634
DN_STEP_CHUNKS = 4
GMLP_STEP_BLOCKS = 4


def _params(*sem):
    return pltpu.CompilerParams(dimension_semantics=sem, vmem_limit_bytes=VMEM_LIMIT)


def _sigmoid(x):
    return 1.0 / (1.0 + jnp.exp2(x * (-LOG2_E)))


def _silu(x):
    return x * _sigmoid(x)


def _gelu(x):
    return 0.5 * x * (1.0 + lax.erf(x * (2.0 ** -0.5)))


def _rmsnorm_kernel(x_ref, g_ref, o_ref):
    x = x_ref[...]
    y = x * lax.rsqrt(jnp.mean(x * x, axis=-1, keepdims=True) + NORM_EPS)
    o_ref[...] = (y * g_ref[...]).astype(o_ref.dtype)


def _rmsnorm(x, g, out_dtype, tr=512):
    s, d = x.shape
    return pl.pallas_call(
        _rmsnorm_kernel,
        grid=(s // tr,),
        in_specs=[pl.BlockSpec((tr, d), lambda i: (i, 0)),
                  pl.BlockSpec((1, d), lambda i: (0, 0))],
        out_specs=pl.BlockSpec((tr, d), lambda i: (i, 0)),
        out_shape=jax.ShapeDtypeStruct((s, d), out_dtype),
        compiler_params=_params("arbitrary"),
        name="rmsnorm",
    )(x, g.reshape(1, d))


def _proj_kernel(*refs, has_residual, w_is_nk, epilogue):
    refs = list(refs)
    a_ref, w_ref = refs[:2]
    pos = 2
    x_ref = cw_ref = ext_ref = None
    if has_residual:
        x_ref = refs[pos]
        pos += 1
    if epilogue == "conv_silu":
        cw_ref = refs[pos]
        pos += 1
    o_ref, wb_ref = refs[pos], refs[pos + 1]
    if epilogue == "conv_silu":
        ext_ref = refs[pos + 2]
    halo = SUBLANES

    @pl.when(pl.program_id(1) == 0)
    def _():
        w = w_ref[...]
        wb_ref[...] = (w.T if w_is_nk else w).astype(wb_ref.dtype)
        if ext_ref is not None:
            ext_ref[0:halo, :] = jnp.zeros((halo, ext_ref.shape[1]), jnp.float32)

    acc = jnp.dot(a_ref[...], wb_ref[...], preferred_element_type=jnp.float32)
    if has_residual:
        acc = acc + x_ref[...]
    if epilogue == "gelu":
        acc = _gelu(acc)
    elif epilogue == "silu":
        acc = _silu(acc)
    elif epilogue == "conv_silu":
        tm = acc.shape[0]
        ext_ref[halo:halo + tm, :] = acc
        conv = cw_ref[CONV_WIDTH - 1:CONV_WIDTH, :] * ext_ref[halo:halo + tm, :]
        for t in range(CONV_WIDTH - 1):
            off = halo - (CONV_WIDTH - 1) + t
            conv = conv + cw_ref[t:t + 1, :] * ext_ref[off:off + tm, :]
        ext_ref[0:halo, :] = ext_ref[tm:tm + halo, :]
        acc = _silu(conv)
    o_ref[...] = acc.astype(o_ref.dtype)


def _proj(a, w3, layer, col0, n, tm, tn, name, residual=None, w_is_nk=False, epilogue=None, conv_w=None):
    m, k = a.shape
    assert col0 % tn == 0 and n % tn == 0 and m % tm == 0
    cb = col0 // tn
    if w_is_nk:
        w_spec = pl.BlockSpec((None, tn, k), lambda j, i: (layer, j + cb, 0))
    else:
        w_spec = pl.BlockSpec((None, k, tn), lambda j, i: (layer, 0, j + cb))
    in_specs = [pl.BlockSpec((tm, k), lambda j, i: (i, 0)), w_spec]
    args = [a, w3]
    scratch = [pltpu.VMEM((k, tn), jnp.bfloat16)]
    if residual is not None:
        in_specs.append(pl.BlockSpec((tm, tn), lambda j, i: (i, j)))
        args.append(residual)
    if epilogue == "conv_silu":
        in_specs.append(pl.BlockSpec((CONV_WIDTH, tn), lambda j, i: (0, j)))
        args.append(conv_w)
        scratch.append(pltpu.VMEM((SUBLANES + tm, tn), jnp.float32))
    return pl.pallas_call(
        functools.partial(_proj_kernel, has_residual=residual is not None, w_is_nk=w_is_nk, epilogue=epilogue),
        grid=(n // tn, m // tm),
        in_specs=in_specs,
        out_specs=pl.BlockSpec((tm, tn), lambda j, i: (i, j)),
        out_shape=jax.ShapeDtypeStruct((m, n), jnp.float32),
        scratch_shapes=scratch,
        compiler_params=_params("arbitrary", "arbitrary"),
        name=name,
    )(*args)


def _cast_kernel(w_ref, o_ref):
    o_ref[...] = w_ref[0].astype(o_ref.dtype)


def _cast_rows(w3, layer, row0, nrows, name, tr=512):
    k = w3.shape[2]
    assert row0 % (2 * SUBLANES) == 0 and nrows % tr == 0
    return pl.pallas_call(
        _cast_kernel,
        grid=(nrows // tr,),
        in_specs=[pl.BlockSpec((pl.Element(1), pl.Element(tr), pl.Element(k)),
                               lambda i: (layer, pl.multiple_of(row0 + i * tr, 2 * SUBLANES), 0))],
        out_specs=pl.BlockSpec((tr, k), lambda i: (i, 0)),
        out_shape=jax.ShapeDtypeStruct((nrows, k), jnp.bfloat16),
        compiler_params=_params("arbitrary"),
        name=name,
    )(w3)


def _gmlp_kernel(u_ref, v_ref, z_ref, lng_ref, lnb_ref, ws_ref, bexp_ref, o_ref):
    row = lax.broadcasted_iota(jnp.int32, (GMLP_BLOCK, GMLP_BLOCK), 0) // CHUNK
    col = lax.broadcasted_iota(jnp.int32, (GMLP_BLOCK, GMLP_BLOCK), 1) // CHUNK
    mask = col <= row
    for blk in range(GMLP_STEP_BLOCKS):
        rs = slice(blk * GMLP_BLOCK, (blk + 1) * GMLP_BLOCK)
        u = u_ref[rs, :]
        v = v_ref[rs, :]
        mu = jnp.mean(v, axis=-1, keepdims=True)
        vc = v - mu
        vn = vc * lax.rsqrt(jnp.mean(vc * vc, axis=-1, keepdims=True) + NORM_EPS)
        vn = vn * lng_ref[...] + lnb_ref[...]
        gate = z_ref[rs, :]
        for g in range(GMLP_GROUPS):
            sl = slice(g * GMLP_GROUP_DIM, (g + 1) * GMLP_GROUP_DIM)
            ws = jnp.where(mask, ws_ref[g], 0.0)
            mixed = jnp.dot(ws, vn[:, sl], preferred_element_type=jnp.float32) + bexp_ref[:, sl]
            o_ref[rs, sl] = (u[:, sl] * mixed * gate[:, sl]).astype(o_ref.dtype)


def _gmlp(uv_act, z_act, ln_g, ln_b, w_s, bexp):
    s = uv_act.shape[0]
    w = GMLP_WIDTH
    tr = GMLP_STEP_BLOCKS * GMLP_BLOCK
    assert s % tr == 0
    row = lambda c: pl.BlockSpec((tr, w), lambda i, c=c: (i, c))
    full2 = lambda shp: pl.BlockSpec(shp, lambda i: (0, 0))
    return pl.pallas_call(
        _gmlp_kernel,
        grid=(s // tr,),
        in_specs=[row(0), row(1), row(0), full2((1, w)), full2((1, w)),
                  pl.BlockSpec((GMLP_GROUPS, GMLP_BLOCK, GMLP_BLOCK), lambda i: (0, 0, 0)),
                  full2((GMLP_BLOCK, w))],
        out_specs=pl.BlockSpec((tr, w), lambda i: (i, 0)),
        out_shape=jax.ShapeDtypeStruct((s, w), jnp.bfloat16),
        compiler_params=_params("arbitrary"),
        name="gmlp_gating",
    )(uv_act, uv_act, z_act, ln_g.reshape(1, w), ln_b.reshape(1, w), w_s, bexp)


def _bmm(a, b):
    return jnp.einsum("hij,hjk->hik", a, b, preferred_element_type=jnp.float32)


def _bmm_nt(a, b):
    return jnp.einsum("hid,hjd->hij", a, b, preferred_element_type=jnp.float32)


def _split3(x):
    bf16, f32 = jnp.bfloat16, jnp.float32
    hi = x.astype(bf16)
    r1 = x - hi.astype(f32)
    mid = r1.astype(bf16)
    lo = (r1 - mid.astype(f32)).astype(bf16)
    return hi, mid, lo


def _deltanet_kernel(qkv_ref, ba_ref, z_ref, alog_ref, dtb_ref, ng_ref, o_ref, rows_ref, state_ref):
    f32, bf16 = jnp.float32, jnp.bfloat16
    i = pl.program_id(0)
    nh = DN_HEADS
    nc = DN_STEP_CHUNKS
    rows = nc * CHUNK
    nb = nc * nh

    @pl.when(i == 0)
    def _():
        state_ref[...] = jnp.zeros(state_ref.shape, f32)

    qkv = qkv_ref[...]

    def split(x2d):
        return jnp.stack([x2d[ci * CHUNK:(ci + 1) * CHUNK, h * DN_HEAD_DIM:(h + 1) * DN_HEAD_DIM]
                          for ci in range(nc) for h in range(nh)], axis=0)

    q = split(qkv[:, :DN_WIDTH])
    k = split(qkv[:, DN_WIDTH:2 * DN_WIDTH])
    v = split(qkv[:, 2 * DN_WIDTH:])
    ones = jnp.ones((DN_HEAD_DIM, DN_HEAD_DIM), f32)

    def sumsq(x):
        sq = (x * x).reshape(nb * CHUNK, DN_HEAD_DIM)
        return jnp.dot(sq, ones, preferred_element_type=f32).reshape(nb, CHUNK, DN_HEAD_DIM)

    q = q * lax.rsqrt((sumsq(q) + NORM_EPS) * float(DN_HEAD_DIM))
    k = k * lax.rsqrt(sumsq(k) + NORM_EPS)

    ba = ba_ref[...]
    beta_all = _sigmoid(ba)
    xg = ba + dtb_ref[...]
    softplus = jnp.maximum(xg, 0.0) + jnp.log1p(jnp.exp(-jnp.abs(xg)))
    g_all = -jnp.exp(alog_ref[...]) * softplus
    r = lax.broadcasted_iota(jnp.int32, (CHUNK, CHUNK), 0)
    c = lax.broadcasted_iota(jnp.int32, (CHUNK, CHUNK), 1)
    incl = r >= c
    strict = r > c
    rr = lax.broadcasted_iota(jnp.int32, (rows, rows), 0)
    cc = lax.broadcasted_iota(jnp.int32, (rows, rows), 1)
    tril = ((rr >= cc) & ((rr // CHUNK) == (cc // CHUNK))).astype(bf16)
    gcum_all = jnp.dot(jnp.concatenate([tril, tril, tril], axis=1),
                       jnp.concatenate(_split3(g_all), axis=0), preferred_element_type=f32)

    lane = lax.broadcasted_iota(jnp.int32, (rows, LANES), 1)
    merged = jnp.where(lane < nh, beta_all, gcum_all)
    for ci in range(nc):
        rows_ref[ci] = merged[ci * CHUNK:(ci + 1) * CHUNK, :].T
    pairs = [(ci, h) for ci in range(nc) for h in range(nh)]
    b_row = jnp.stack([rows_ref[ci, h:h + 1, :] for ci, h in pairs], axis=0)
    gc_row = jnp.stack([rows_ref[ci, nh + h:nh + h + 1, :] for ci, h in pairs], axis=0)
    b_col = jnp.stack([jnp.broadcast_to(beta_all[ci * CHUNK:(ci + 1) * CHUNK, h:h + 1], (CHUNK, LANES))
                       for ci, h in pairs], axis=0)
    gc_col = jnp.stack([jnp.broadcast_to(gcum_all[ci * CHUNK:(ci + 1) * CHUNK, nh + h:nh + h + 1], (CHUNK, LANES))
                        for ci, h in pairs], axis=0)

    gam = jnp.exp(gc_col)
    gc_last = gc_col[:, CHUNK - 1:CHUNK, :]
    g_end = jnp.exp(gc_last)
    kdb = jnp.exp(gc_last - gc_col) * b_col
    decb = jnp.exp(gc_col[:, :, :CHUNK] - gc_row) * b_row

    a = jnp.where(strict[None], _bmm_nt(k, k) * decb, 0.0)
    eye_c = (r == c).astype(f32)[None]
    same_blk = ((r // SUBLANES) == (c // SUBLANES))[None]

    def inv_nilpotent8(p):
        pb = (-p).astype(bf16)
        t = eye_c - p
        for _ in range(2):
            pb = _bmm(pb, pb).astype(bf16)
            t = t + _bmm(t.astype(bf16), pb)
        return t

    t_d = inv_nilpotent8(jnp.where(same_blk, a, 0.0))
    t_db = t_d.astype(bf16)
    t_n = inv_nilpotent8(_bmm(t_db, jnp.where(same_blk, 0.0, a).astype(bf16)))
    tinv = _bmm(t_n.astype(bf16), t_db)
    rhs = jnp.concatenate([v, gam * k], axis=-1)
    sol = _bmm(tinv, rhs)
    u_s = sol[:, :, :DN_HEAD_DIM]
    kc_s = sol[:, :, DN_HEAD_DIM:]

    aqkb = jnp.where(incl[None], _bmm_nt(q, k) * decb, 0.0)
    lhs = jnp.concatenate([kc_s, q * gam], axis=1)
    kd = k * kdb

    state = state_ref[...]
    outs = []
    for ci in range(nc):
        sl = slice(ci * nh, (ci + 1) * nh)
        prod = _bmm(lhs[sl], state)
        w_s = u_s[sl] - prod[:, :CHUNK]
        outs.append(prod[:, CHUNK:] + _bmm(aqkb[sl], w_s))
        state = g_end[sl] * state + _bmm(jnp.swapaxes(kd[sl], 1, 2), w_s)
    state_ref[...] = state

    o = jnp.concatenate(outs, axis=0)
    o = o * lax.rsqrt(sumsq(o) * (1.0 / DN_HEAD_DIM) + NORM_EPS) * ng_ref[...]
    o2d = jnp.concatenate([jnp.concatenate([o[ci * nh + h] for h in range(nh)], axis=-1)
                           for ci in range(nc)], axis=0)
    o_ref[...] = (o2d * z_ref[...]).astype(o_ref.dtype)


def _deltanet(qkv_act, ba, z_act, alog_row, dtb_row, ng):
    s = qkv_act.shape[0]
    rows = DN_STEP_CHUNKS * CHUNK
    assert s % rows == 0
    qkv_blk = 3 * DN_WIDTH
    full2 = lambda shp: pl.BlockSpec(shp, lambda i: (0, 0))
    return pl.pallas_call(
        _deltanet_kernel,
        grid=(s // rows,),
        in_specs=[pl.BlockSpec((rows, qkv_blk), lambda i: (i, 0)),
                  pl.BlockSpec((rows, LANES), lambda i: (i, 0)),
                  pl.BlockSpec((rows, DN_WIDTH), lambda i: (i, 0)),
                  full2((1, LANES)), full2((1, LANES)), full2((1, DN_HEAD_DIM))],
        out_specs=pl.BlockSpec((rows, DN_WIDTH), lambda i: (i, 0)),
        out_shape=jax.ShapeDtypeStruct((s, DN_WIDTH), jnp.bfloat16),
        scratch_shapes=[pltpu.VMEM((DN_STEP_CHUNKS, LANES, CHUNK), jnp.float32),
                        pltpu.VMEM((DN_HEADS, DN_HEAD_DIM, DN_HEAD_DIM), jnp.float32)],
        compiler_params=_params("arbitrary"),
        name="gated_deltanet",
    )(qkv_act, ba, z_act, alog_row, dtb_row, ng.reshape(1, DN_HEAD_DIM))


def _merge_kernel(h_ref, agm_ref, adn_ref, wggm_ref, wgdn_ref, wbgm_ref, wbdn_ref, o_ref):
    f32 = jnp.float32
    h = h_ref[...]
    nt = (((1,), (1,)), ((), ()))
    gate_gm = _sigmoid(lax.dot_general(h, wggm_ref[...], nt, preferred_element_type=f32))
    y_gm = jnp.dot(agm_ref[...], wbgm_ref[...], preferred_element_type=f32)
    acc = gate_gm * y_gm
    gate_dn = _sigmoid(lax.dot_general(h, wgdn_ref[...], nt, preferred_element_type=f32))
    y_dn = jnp.dot(adn_ref[...], wbdn_ref[...], preferred_element_type=f32)
    o_ref[...] = (acc + gate_dn * y_dn).astype(o_ref.dtype)


def _merge(h, a_gm, a_dn, w_gate_t, w_br_gm, w_br_dn, tm=512, tn=512):
    s, d = h.shape
    w = a_gm.shape[1]
    ngate = d // tn
    return pl.pallas_call(
        _merge_kernel,
        grid=(s // tm, d // tn),
        in_specs=[pl.BlockSpec((tm, d), lambda i, j: (i, 0)),
                  pl.BlockSpec((tm, w), lambda i, j: (i, 0)),
                  pl.BlockSpec((tm, w), lambda i, j: (i, 0)),
                  pl.BlockSpec((tn, d), lambda i, j: (j, 0)),
                  pl.BlockSpec((tn, d), lambda i, j: (j + ngate, 0)),
                  pl.BlockSpec((w, tn), lambda i, j: (0, j)),
                  pl.BlockSpec((w, tn), lambda i, j: (0, j))],
        out_specs=pl.BlockSpec((tm, tn), lambda i, j: (i, j)),
        out_shape=jax.ShapeDtypeStruct((s, d), jnp.bfloat16),
        compiler_params=_params("arbitrary", "arbitrary"),
        name="gated_merge",
    )(h, a_gm, a_dn, w_gate_t, w_gate_t, w_br_gm, w_br_dn)


def kernel(x, norm_g, w_in, conv_w, a_log, dt_bias, dn_norm_g, ln_g, ln_b, w_s, b_s,
           w_br_gmlp, w_br_dn, w_out, final_g):
    bsz, seq, d = x.shape
    assert bsz == 1, "the conv halo and the delta-rule state are carried across row tiles of one sequence"
    bf16 = jnp.bfloat16
    xs = x.reshape(bsz * seq, d)
    head_pad = jnp.zeros((DN_HEADS,), jnp.float32)
    tail_pad = jnp.zeros((LANES - N_BA,), jnp.float32)
    w_in_t = jnp.swapaxes(w_in, 1, 2)
    for l in range(DEPTH):
        w_gate_t = _cast_rows(w_in_t, l, N_MAIN + N_BA, 2 * d, "gate_weight_cast")
        alog_row = jnp.concatenate([head_pad, a_log[l], tail_pad]).reshape(1, LANES)
        dtb_row = jnp.concatenate([head_pad, dt_bias[l], tail_pad]).reshape(1, LANES)
        bexp = jnp.repeat(b_s[l].T, GMLP_GROUP_DIM, axis=1)

        h = _rmsnorm(xs, norm_g[l], bf16)
        gw, dw = GMLP_WIDTH, DN_WIDTH
        uv_act = _proj(h, w_in_t, l, 0, 2 * gw, 1024, 512, "in_proj_uv", w_is_nk=True, epilogue="gelu")
        zgm_act = _proj(h, w_in_t, l, 2 * gw, gw, 1024, 512, "in_proj_zgm", w_is_nk=True, epilogue="silu")
        qkv_act = _proj(h, w_in_t, l, 3 * gw, 3 * dw, 1024, 512, "in_proj_qkv", w_is_nk=True,
                        epilogue="conv_silu", conv_w=conv_w[l])
        zdn_act = _proj(h, w_in_t, l, 3 * gw + 3 * dw, dw, 1024, 512, "in_proj_zdn", w_is_nk=True, epilogue="silu")
        ba = _proj(h, w_in_t, l, N_MAIN, LANES, 1024, LANES, "in_proj_ba", w_is_nk=True)
        a_gm = _gmlp(uv_act, zgm_act, ln_g[l], ln_b[l], w_s[l], bexp)
        a_dn = _deltanet(qkv_act, ba, zdn_act, alog_row, dtb_row, dn_norm_g[l])
        merged = _merge(h, a_gm, a_dn, w_gate_t, w_br_gmlp[l].astype(bf16), w_br_dn[l].astype(bf16))
        xs = _proj(merged, w_out, l, 0, d, 1024, 512, "out_proj_residual", residual=xs)
    out = _rmsnorm(xs, final_g, jnp.float32)
    return out.reshape(bsz, seq, d)
```

```python
import functools

import jax
import jax.numpy as jnp
from jax import lax
from jax.experimental import pallas as pl
from jax.experimental.pallas import tpu as pltpu

D_MODEL = 4096
DEPTH = 4
CHUNK = 64
GMLP_BLOCK = 128
GMLP_WIDTH = D_MODEL // 2
GMLP_GROUP_DIM = 128
GMLP_GROUPS = GMLP_WIDTH // GMLP_GROUP_DIM
DN_WIDTH = D_MODEL // 2
DN_HEAD_DIM = 128
DN_HEADS = DN_WIDTH // DN_HEAD_DIM
CONV_WIDTH = 4
NORM_EPS = 1e-6
LANES = 128
SUBLANES = 8
N_MAIN = 3 * GMLP_WIDTH + 4 * DN_WIDTH
N_BA = 2 * DN_HEADS
VMEM_LIMIT = 60 * 1024 * 1024

LOG2_E = 1.4426950408889634
DN_STEP_CHUNKS = 4
GMLP_STEP_BLOCKS = 4


def _params(*sem):
    return pltpu.CompilerParams(dimension_semantics=sem, vmem_limit_bytes=VMEM_LIMIT)


def _sigmoid(x):
    return 1.0 / (1.0 + jnp.exp2(x * (-LOG2_E)))


def _silu(x):
    return x * _sigmoid(x)


def _gelu(x):
    return 0.5 * x * (1.0 + lax.erf(x * (2.0 ** -0.5)))


def _rmsnorm_kernel(x_ref, g_ref, o_ref):
    x = x_ref[...]
    y = x * lax.rsqrt(jnp.mean(x * x, axis=-1, keepdims=True) + NORM_EPS)
    o_ref[...] = (y * g_ref[...]).astype(o_ref.dtype)


def _rmsnorm(x, g, out_dtype, tr=512):
    s, d = x.shape
    return pl.pallas_call(
        _rmsnorm_kernel,
        grid=(s // tr,),
        in_specs=[pl.BlockSpec((tr, d), lambda i: (i, 0)),
                  pl.BlockSpec((1, d), lambda i: (0, 0))],
        out_specs=pl.BlockSpec((tr, d), lambda i: (i, 0)),
        out_shape=jax.ShapeDtypeStruct((s, d), out_dtype),
        compiler_params=_params("arbitrary"),
        name="rmsnorm",
    )(x, g.reshape(1, d))


def _proj_kernel(*refs, has_residual, w_is_nk, epilogue):
    refs = list(refs)
    a_ref, w_ref = refs[:2]
    pos = 2
    x_ref = cw_ref = ext_ref = None
    if has_residual:
        x_ref = refs[pos]
        pos += 1
    if epilogue == "conv_silu":
        cw_ref = refs[pos]
        pos += 1
    o_ref, wb_ref = refs[pos], refs[pos + 1]
    if epilogue == "conv_silu":
        ext_ref = refs[pos + 2]
    halo = SUBLANES

    @pl.when(pl.program_id(1) == 0)
    def _():
        w = w_ref[...]
        wb_ref[...] = (w.T if w_is_nk else w).astype(wb_ref.dtype)
        if ext_ref is not None:
            ext_ref[0:halo, :] = jnp.zeros((halo, ext_ref.shape[1]), jnp.float32)

    acc = jnp.dot(a_ref[...], wb_ref[...], preferred_element_type=jnp.float32)
    if has_residual:
        acc = acc + x_ref[...]
    if epilogue == "gelu":
        acc = _gelu(acc)
    elif epilogue == "silu":
        acc = _silu(acc)
    elif epilogue == "conv_silu":
        tm = acc.shape[0]
        ext_ref[halo:halo + tm, :] = acc
        conv = cw_ref[CONV_WIDTH - 1:CONV_WIDTH, :] * ext_ref[halo:halo + tm, :]
        for t in range(CONV_WIDTH - 1):
            off = halo - (CONV_WIDTH - 1) + t
            conv = conv + cw_ref[t:t + 1, :] * ext_ref[off:off + tm, :]
        ext_ref[0:halo, :] = ext_ref[tm:tm + halo, :]
        acc = _silu(conv)
    o_ref[...] = acc.astype(o_ref.dtype)


def _proj(a, w3, layer, col0, n, tm, tn, name, residual=None, w_is_nk=False, epilogue=None, conv_w=None):
    m, k = a.shape
    assert col0 % tn == 0 and n % tn == 0 and m % tm == 0
    cb = col0 // tn
    if w_is_nk:
        w_spec = pl.BlockSpec((None, tn, k), lambda j, i: (layer, j + cb, 0))
    else:
        w_spec = pl.BlockSpec((None, k, tn), lambda j, i: (layer, 0, j + cb))
    in_specs = [pl.BlockSpec((tm, k), lambda j, i: (i, 0)), w_spec]
    args = [a, w3]
    scratch = [pltpu.VMEM((k, tn), jnp.bfloat16)]
    if residual is not None:
        in_specs.append(pl.BlockSpec((tm, tn), lambda j, i: (i, j)))
        args.append(residual)
    if epilogue == "conv_silu":
        in_specs.append(pl.BlockSpec((CONV_WIDTH, tn), lambda j, i: (0, j)))
        args.append(conv_w)
        scratch.append(pltpu.VMEM((SUBLANES + tm, tn), jnp.float32))
    return pl.pallas_call(
        functools.partial(_proj_kernel, has_residual=residual is not None, w_is_nk=w_is_nk, epilogue=epilogue),
        grid=(n // tn, m // tm),
        in_specs=in_specs,
        out_specs=pl.BlockSpec((tm, tn), lambda j, i: (i, j)),
        out_shape=jax.ShapeDtypeStruct((m, n), jnp.float32),
        scratch_shapes=scratch,
        compiler_params=_params("arbitrary", "arbitrary"),
        name=name,
    )(*args)


def _cast_kernel(w_ref, o_ref):
    o_ref[...] = w_ref[0].astype(o_ref.dtype)


def _cast_rows(w3, layer, row0, nrows, name, tr=512):
    k = w3.shape[2]
    assert row0 % (2 * SUBLANES) == 0 and nrows % tr == 0
    return pl.pallas_call(
        _cast_kernel,
        grid=(nrows // tr,),
        in_specs=[pl.BlockSpec((pl.Element(1), pl.Element(tr), pl.Element(k)),
                               lambda i: (layer, pl.multiple_of(row0 + i * tr, 2 * SUBLANES), 0))],
        out_specs=pl.BlockSpec((tr, k), lambda i: (i, 0)),
        out_shape=jax.ShapeDtypeStruct((nrows, k), jnp.bfloat16),
        compiler_params=_params("arbitrary"),
        name=name,
    )(w3)


def _gmlp_kernel(u_ref, v_ref, z_ref, lng_ref, lnb_ref, ws_ref, bexp_ref, o_ref):
    row = lax.broadcasted_iota(jnp.int32, (GMLP_BLOCK, GMLP_BLOCK), 0) // CHUNK
    col = lax.broadcasted_iota(jnp.int32, (GMLP_BLOCK, GMLP_BLOCK), 1) // CHUNK
    mask = col <= row
    for blk in range(GMLP_STEP_BLOCKS):
        rs = slice(blk * GMLP_BLOCK, (blk + 1) * GMLP_BLOCK)
        u = u_ref[rs, :]
        v = v_ref[rs, :]
        mu = jnp.mean(v, axis=-1, keepdims=True)
        vc = v - mu
        vn = vc * lax.rsqrt(jnp.mean(vc * vc, axis=-1, keepdims=True) + NORM_EPS)
        vn = vn * lng_ref[...] + lnb_ref[...]
        gate = z_ref[rs, :]
        for g in range(GMLP_GROUPS):
            sl = slice(g * GMLP_GROUP_DIM, (g + 1) * GMLP_GROUP_DIM)
            ws = jnp.where(mask, ws_ref[g], 0.0)
            mixed = jnp.dot(ws, vn[:, sl], preferred_element_type=jnp.float32) + bexp_ref[:, sl]
            o_ref[rs, sl] = (u[:, sl] * mixed * gate[:, sl]).astype(o_ref.dtype)


def _gmlp(uv_act, z_act, ln_g, ln_b, w_s, bexp):
    s = uv_act.shape[0]
    w = GMLP_WIDTH
    tr = GMLP_STEP_BLOCKS * GMLP_BLOCK
    assert s % tr == 0
    row = lambda c: pl.BlockSpec((tr, w), lambda i, c=c: (i, c))
    full2 = lambda shp: pl.BlockSpec(shp, lambda i: (0, 0))
    return pl.pallas_call(
        _gmlp_kernel,
        grid=(s // tr,),
        in_specs=[row(0), row(1), row(0), full2((1, w)), full2((1, w)),
                  pl.BlockSpec((GMLP_GROUPS, GMLP_BLOCK, GMLP_BLOCK), lambda i: (0, 0, 0)),
                  full2((GMLP_BLOCK, w))],
        out_specs=pl.BlockSpec((tr, w), lambda i: (i, 0)),
        out_shape=jax.ShapeDtypeStruct((s, w), jnp.bfloat16),
        compiler_params=_params("arbitrary"),
        name="gmlp_gating",
    )(uv_act, uv_act, z_act, ln_g.reshape(1, w), ln_b.reshape(1, w), w_s, bexp)


def _bmm(a, b):
    return jnp.einsum("hij,hjk->hik", a, b, preferred_element_type=jnp.float32)


def _bmm_nt(a, b):
    return jnp.einsum("hid,hjd->hij", a, b, preferred_element_type=jnp.float32)


def _split3(x):
    bf16, f32 = jnp.bfloat16, jnp.float32
    hi = x.astype(bf16)
    r1 = x - hi.astype(f32)
    mid = r1.astype(bf16)
    lo = (r1 - mid.astype(f32)).astype(bf16)
    return hi, mid, lo


def _deltanet_kernel(qkv_ref, ba_ref, z_ref, alog_ref, dtb_ref, ng_ref, o_ref, rows_ref, state_ref):
    f32, bf16 = jnp.float32, jnp.bfloat16
    i = pl.program_id(0)
    nh = DN_HEADS
    nc = DN_STEP_CHUNKS
    rows = nc * CHUNK
    nb = nc * nh

    @pl.when(i == 0)
    def _():
        state_ref[...] = jnp.zeros(state_ref.shape, f32)

    qkv = qkv_ref[...]

    def split(x2d):
        return jnp.stack([x2d[ci * CHUNK:(ci + 1) * CHUNK, h * DN_HEAD_DIM:(h + 1) * DN_HEAD_DIM]
                          for ci in range(nc) for h in range(nh)], axis=0)

    q = split(qkv[:, :DN_WIDTH])
    k = split(qkv[:, DN_WIDTH:2 * DN_WIDTH])
    v = split(qkv[:, 2 * DN_WIDTH:])
    ones = jnp.ones((DN_HEAD_DIM, DN_HEAD_DIM), f32)

    def sumsq(x):
        sq = (x * x).reshape(nb * CHUNK, DN_HEAD_DIM)
        return jnp.dot(sq, ones, preferred_element_type=f32).reshape(nb, CHUNK, DN_HEAD_DIM)

    q = q * lax.rsqrt((sumsq(q) + NORM_EPS) * float(DN_HEAD_DIM))
    k = k * lax.rsqrt(sumsq(k) + NORM_EPS)

    ba = ba_ref[...]
    beta_all = _sigmoid(ba)
    xg = ba + dtb_ref[...]
    softplus = jnp.maximum(xg, 0.0) + jnp.log1p(jnp.exp(-jnp.abs(xg)))
    g_all = -jnp.exp(alog_ref[...]) * softplus
    r = lax.broadcasted_iota(jnp.int32, (CHUNK, CHUNK), 0)
    c = lax.broadcasted_iota(jnp.int32, (CHUNK, CHUNK), 1)
    incl = r >= c
    strict = r > c
    rr = lax.broadcasted_iota(jnp.int32, (rows, rows), 0)
    cc = lax.broadcasted_iota(jnp.int32, (rows, rows), 1)
    tril = ((rr >= cc) & ((rr // CHUNK) == (cc // CHUNK))).astype(bf16)
    gcum_all = jnp.dot(jnp.concatenate([tril, tril, tril], axis=1),
                       jnp.concatenate(_split3(g_all), axis=0), preferred_element_type=f32)

    lane = lax.broadcasted_iota(jnp.int32, (rows, LANES), 1)
    merged = jnp.where(lane < nh, beta_all, gcum_all)
    for ci in range(nc):
        rows_ref[ci] = merged[ci * CHUNK:(ci + 1) * CHUNK, :].T
    pairs = [(ci, h) for ci in range(nc) for h in range(nh)]
    b_row = jnp.stack([rows_ref[ci, h:h + 1, :] for ci, h in pairs], axis=0)
    gc_row = jnp.stack([rows_ref[ci, nh + h:nh + h + 1, :] for ci, h in pairs], axis=0)
    b_col = jnp.stack([jnp.broadcast_to(beta_all[ci * CHUNK:(ci + 1) * CHUNK, h:h + 1], (CHUNK, LANES))
                       for ci, h in pairs], axis=0)
    gc_col = jnp.stack([jnp.broadcast_to(gcum_all[ci * CHUNK:(ci + 1) * CHUNK, nh + h:nh + h + 1], (CHUNK, LANES))
                        for ci, h in pairs], axis=0)

    gam = jnp.exp(gc_col)
    gc_last = gc_col[:, CHUNK - 1:CHUNK, :]
    g_end = jnp.exp(gc_last)
    kdb = jnp.exp(gc_last - gc_col) * b_col
    decb = jnp.exp(gc_col[:, :, :CHUNK] - gc_row) * b_row

    a = jnp.where(strict[None], _bmm_nt(k, k) * decb, 0.0)
    eye_c = (r == c).astype(f32)[None]
    same_blk = ((r // SUBLANES) == (c // SUBLANES))[None]

    def inv_nilpotent8(p):
        pb = (-p).astype(bf16)
        t = eye_c - p
        for _ in range(2):
            pb = _bmm(pb, pb).astype(bf16)
            t = t + _bmm(t.astype(bf16), pb)
        return t

    t_d = inv_nilpotent8(jnp.where(same_blk, a, 0.0))
    t_db = t_d.astype(bf16)
    t_n = inv_nilpotent8(_bmm(t_db, jnp.where(same_blk, 0.0, a).astype(bf16)))
    tinv = _bmm(t_n.astype(bf16), t_db)
    rhs = jnp.concatenate([v, gam * k], axis=-1)
    sol = _bmm(tinv, rhs)
    u_s = sol[:, :, :DN_HEAD_DIM]
    kc_s = sol[:, :, DN_HEAD_DIM:]

    aqkb = jnp.where(incl[None], _bmm_nt(q, k) * decb, 0.0)
    lhs = jnp.concatenate([kc_s, q * gam], axis=1)
    kd = k * kdb

    state = state_ref[...]
    outs = []
    for ci in range(nc):
        sl = slice(ci * nh, (ci + 1) * nh)
        prod = _bmm(lhs[sl], state)
        w_s = u_s[sl] - prod[:, :CHUNK]
        outs.append(prod[:, CHUNK:] + _bmm(aqkb[sl], w_s))
        state = g_end[sl] * state + _bmm(jnp.swapaxes(kd[sl], 1, 2), w_s)
    state_ref[...] = state

    o = jnp.concatenate(outs, axis=0)
    o = o * lax.rsqrt(sumsq(o) * (1.0 / DN_HEAD_DIM) + NORM_EPS) * ng_ref[...]
    o2d = jnp.concatenate([jnp.concatenate([o[ci * nh + h] for h in range(nh)], axis=-1)
                           for ci in range(nc)], axis=0)
    o_ref[...] = (o2d * z_ref[...]).astype(o_ref.dtype)


def _deltanet(qkv_act, ba, z_act, alog_row, dtb_row, ng):
    s = qkv_act.shape[0]
    rows = DN_STEP_CHUNKS * CHUNK
    assert s % rows == 0
    qkv_blk = 3 * DN_WIDTH
    full2 = lambda shp: pl.BlockSpec(shp, lambda i: (0, 0))
    return pl.pallas_call(
        _deltanet_kernel,
        grid=(s // rows,),
        in_specs=[pl.BlockSpec((rows, qkv_blk), lambda i: (i, 0)),
                  pl.BlockSpec((rows, LANES), lambda i: (i, 0)),
                  pl.BlockSpec((rows, DN_WIDTH), lambda i: (i, 0)),
                  full2((1, LANES)), full2((1, LANES)), full2((1, DN_HEAD_DIM))],
        out_specs=pl.BlockSpec((rows, DN_WIDTH), lambda i: (i, 0)),
        out_shape=jax.ShapeDtypeStruct((s, DN_WIDTH), jnp.bfloat16),
        scratch_shapes=[pltpu.VMEM((DN_STEP_CHUNKS, LANES, CHUNK), jnp.float32),
                        pltpu.VMEM((DN_HEADS, DN_HEAD_DIM, DN_HEAD_DIM), jnp.float32)],
        compiler_params=_params("arbitrary"),
        name="gated_deltanet",
    )(qkv_act, ba, z_act, alog_row, dtb_row, ng.reshape(1, DN_HEAD_DIM))


def _merge_kernel(h_ref, agm_ref, adn_ref, wggm_ref, wgdn_ref, wbgm_ref, wbdn_ref, o_ref):
    f32 = jnp.float32
    h = h_ref[...]
    nt = (((1,), (1,)), ((), ()))
    gate_gm = _sigmoid(lax.dot_general(h, wggm_ref[...], nt, preferred_element_type=f32))
    y_gm = jnp.dot(agm_ref[...], wbgm_ref[...], preferred_element_type=f32)
    acc = gate_gm * y_gm
    gate_dn = _sigmoid(lax.dot_general(h, wgdn_ref[...], nt, preferred_element_type=f32))
    y_dn = jnp.dot(adn_ref[...], wbdn_ref[...], preferred_element_type=f32)
    o_ref[...] = (acc + gate_dn * y_dn).astype(o_ref.dtype)


def _merge(h, a_gm, a_dn, w_gate_t, w_br_gm, w_br_dn, tm=1024, tn=256):
    s, d = h.shape
    w = a_gm.shape[1]
    ngate = d // tn
    return pl.pallas_call(
        _merge_kernel,
        grid=(s // tm, d // tn),
        in_specs=[pl.BlockSpec((tm, d), lambda i, j: (i, 0)),
                  pl.BlockSpec((tm, w), lambda i, j: (i, 0)),
                  pl.BlockSpec((tm, w), lambda i, j: (i, 0)),
                  pl.BlockSpec((tn, d), lambda i, j: (j, 0)),
                  pl.BlockSpec((tn, d), lambda i, j: (j + ngate, 0)),
                  pl.BlockSpec((w, tn), lambda i, j: (0, j)),
                  pl.BlockSpec((w, tn), lambda i, j: (0, j))],
        out_specs=pl.BlockSpec((tm, tn), lambda i, j: (i, j)),
        out_shape=jax.ShapeDtypeStruct((s, d), jnp.bfloat16),
        compiler_params=_params("arbitrary", "arbitrary"),
        name="gated_merge",
    )(h, a_gm, a_dn, w_gate_t, w_gate_t, w_br_gm, w_br_dn)


def kernel(x, norm_g, w_in, conv_w, a_log, dt_bias, dn_norm_g, ln_g, ln_b, w_s, b_s,
           w_br_gmlp, w_br_dn, w_out, final_g):
    bsz, seq, d = x.shape
    assert bsz == 1, "the conv halo and the delta-rule state are carried across row tiles of one sequence"
    bf16 = jnp.bfloat16
    xs = x.reshape(bsz * seq, d)
    head_pad = jnp.zeros((DN_HEADS,), jnp.float32)
    tail_pad = jnp.zeros((LANES - N_BA,), jnp.float32)
    w_in_t = jnp.swapaxes(w_in, 1, 2)
    for l in range(DEPTH):
        w_gate_t = _cast_rows(w_in_t, l, N_MAIN + N_BA, 2 * d, "gate_weight_cast")
        alog_row = jnp.concatenate([head_pad, a_log[l], tail_pad]).reshape(1, LANES)
        dtb_row = jnp.concatenate([head_pad, dt_bias[l], tail_pad]).reshape(1, LANES)
        bexp = jnp.repeat(b_s[l].T, GMLP_GROUP_DIM, axis=1)

        h = _rmsnorm(xs, norm_g[l], bf16)
        gw, dw = GMLP_WIDTH, DN_WIDTH
        uv_act = _proj(h, w_in_t, l, 0, 2 * gw, 512, 1024, "in_proj_uv", w_is_nk=True, epilogue="gelu")
        zgm_act = _proj(h, w_in_t, l, 2 * gw, gw, 512, 1024, "in_proj_zgm", w_is_nk=True, epilogue="silu")
        qkv_act = _proj(h, w_in_t, l, 3 * gw, 3 * dw, 512, 1024, "in_proj_qkv", w_is_nk=True,
                        epilogue="conv_silu", conv_w=conv_w[l])
        zdn_act = _proj(h, w_in_t, l, 3 * gw + 3 * dw, dw, 512, 1024, "in_proj_zdn", w_is_nk=True, epilogue="silu")
        ba = _proj(h, w_in_t, l, N_MAIN, LANES, 1024, LANES, "in_proj_ba", w_is_nk=True)
        a_gm = _gmlp(uv_act, zgm_act, ln_g[l], ln_b[l], w_s[l], bexp)
        a_dn = _deltanet(qkv_act, ba, zdn_act, alog_row, dtb_row, dn_norm_g[l])
        merged = _merge(h, a_gm, a_dn, w_gate_t, w_br_gmlp[l].astype(bf16), w_br_dn[l].astype(bf16))
        xs = _proj(merged, w_out, l, 0, d, 512, 1024, "out_proj_residual", residual=xs)
    out = _rmsnorm(xs, final_g, jnp.float32)
    return out.reshape(bsz, seq, d)
```

```python
import functools

import jax
import jax.numpy as jnp
from jax import lax
from jax.experimental import pallas as pl
from jax.experimental.pallas import tpu as pltpu

D_MODEL = 4096
DEPTH = 4
CHUNK = 64
GMLP_BLOCK = 128
GMLP_WIDTH = D_MODEL // 2
GMLP_GROUP_DIM = 128
GMLP_GROUPS = GMLP_WIDTH // GMLP_GROUP_DIM
DN_WIDTH = D_MODEL // 2
DN_HEAD_DIM = 128
DN_HEADS = DN_WIDTH // DN_HEAD_DIM
CONV_WIDTH = 4
NORM_EPS = 1e-6
LANES = 128
SUBLANES = 8
N_MAIN = 3 * GMLP_WIDTH + 4 * DN_WIDTH
N_BA = 2 * DN_HEADS
VMEM_LIMIT = 60 * 1024 * 1024

LOG2_E = 1.4426950408889634
DN_STEP_CHUNKS = 4
GMLP_STEP_BLOCKS = 4


def _params(*sem):
    return pltpu.CompilerParams(dimension_semantics=sem, vmem_limit_bytes=VMEM_LIMIT)


def _sigmoid(x):
    return 1.0 / (1.0 + jnp.exp2(x * (-LOG2_E)))


def _silu(x):
    return x * _sigmoid(x)


def _gelu(x):
    return 0.5 * x * (1.0 + lax.erf(x * (2.0 ** -0.5)))


def _rmsnorm_kernel(x_ref, g_ref, o_ref):
    x = x_ref[...]
    y = x * lax.rsqrt(jnp.mean(x * x, axis=-1, keepdims=True) + NORM_EPS)
    o_ref[...] = (y * g_ref[...]).astype(o_ref.dtype)


def _rmsnorm(x, g, out_dtype, tr=512):
    s, d = x.shape
    return pl.pallas_call(
        _rmsnorm_kernel,
        grid=(s // tr,),
        in_specs=[pl.BlockSpec((tr, d), lambda i: (i, 0)),
                  pl.BlockSpec((1, d), lambda i: (0, 0))],
        out_specs=pl.BlockSpec((tr, d), lambda i: (i, 0)),
        out_shape=jax.ShapeDtypeStruct((s, d), out_dtype),
        compiler_params=_params("arbitrary"),
        name="rmsnorm",
    )(x, g.reshape(1, d))


def _proj_kernel(*refs, has_residual, w_is_nk, epilogue):
    refs = list(refs)
    a_ref, w_ref = refs[:2]
    pos = 2
    x_ref = cw_ref = ext_ref = None
    if has_residual:
        x_ref = refs[pos]
        pos += 1
    if epilogue == "conv_silu":
        cw_ref = refs[pos]
        pos += 1
    o_ref, wb_ref = refs[pos], refs[pos + 1]
    if epilogue == "conv_silu":
        ext_ref = refs[pos + 2]
    halo = SUBLANES

    @pl.when(pl.program_id(1) == 0)
    def _():
        w = w_ref[...]
        wb_ref[...] = (w.T if w_is_nk else w).astype(wb_ref.dtype)
        if ext_ref is not None:
            ext_ref[0:halo, :] = jnp.zeros((halo, ext_ref.shape[1]), jnp.float32)

    acc = jnp.dot(a_ref[...], wb_ref[...], preferred_element_type=jnp.float32)
    if has_residual:
        acc = acc + x_ref[...]
    if epilogue == "gelu":
        acc = _gelu(acc)
    elif epilogue == "silu":
        acc = _silu(acc)
    elif epilogue == "conv_silu":
        tm = acc.shape[0]
        ext_ref[halo:halo + tm, :] = acc
        conv = cw_ref[CONV_WIDTH - 1:CONV_WIDTH, :] * ext_ref[halo:halo + tm, :]
        for t in range(CONV_WIDTH - 1):
            off = halo - (CONV_WIDTH - 1) + t
            conv = conv + cw_ref[t:t + 1, :] * ext_ref[off:off + tm, :]
        ext_ref[0:halo, :] = ext_ref[tm:tm + halo, :]
        acc = _silu(conv)
    o_ref[...] = acc.astype(o_ref.dtype)


def _proj(a, w3, layer, col0, n, tm, tn, name, residual=None, w_is_nk=False, epilogue=None, conv_w=None):
    m, k = a.shape
    assert col0 % tn == 0 and n % tn == 0 and m % tm == 0
    cb = col0 // tn
    if w_is_nk:
        w_spec = pl.BlockSpec((None, tn, k), lambda j, i: (layer, j + cb, 0))
    else:
        w_spec = pl.BlockSpec((None, k, tn), lambda j, i: (layer, 0, j + cb))
    in_specs = [pl.BlockSpec((tm, k), lambda j, i: (i, 0)), w_spec]
    args = [a, w3]
    scratch = [pltpu.VMEM((k, tn), jnp.bfloat16)]
    if residual is not None:
        in_specs.append(pl.BlockSpec((tm, tn), lambda j, i: (i, j)))
        args.append(residual)
    if epilogue == "conv_silu":
        in_specs.append(pl.BlockSpec((CONV_WIDTH, tn), lambda j, i: (0, j)))
        args.append(conv_w)
        scratch.append(pltpu.VMEM((SUBLANES + tm, tn), jnp.float32))
    return pl.pallas_call(
        functools.partial(_proj_kernel, has_residual=residual is not None, w_is_nk=w_is_nk, epilogue=epilogue),
        grid=(n // tn, m // tm),
        in_specs=in_specs,
        out_specs=pl.BlockSpec((tm, tn), lambda j, i: (i, j)),
        out_shape=jax.ShapeDtypeStruct((m, n), jnp.float32),
        scratch_shapes=scratch,
        compiler_params=_params("arbitrary", "arbitrary"),
        name=name,
    )(*args)


def _cast_kernel(w_ref, o_ref):
    o_ref[...] = w_ref[0].astype(o_ref.dtype)


def _cast_rows(w3, layer, row0, nrows, name, tr=512):
    k = w3.shape[2]
    assert row0 % (2 * SUBLANES) == 0 and nrows % tr == 0
    return pl.pallas_call(
        _cast_kernel,
        grid=(nrows // tr,),
        in_specs=[pl.BlockSpec((pl.Element(1), pl.Element(tr), pl.Element(k)),
                               lambda i: (layer, pl.multiple_of(row0 + i * tr, 2 * SUBLANES), 0))],
        out_specs=pl.BlockSpec((tr, k), lambda i: (i, 0)),
        out_shape=jax.ShapeDtypeStruct((nrows, k), jnp.bfloat16),
        compiler_params=_params("arbitrary"),
        name=name,
    )(w3)


def _gmlp_kernel(u_ref, v_ref, z_ref, lng_ref, lnb_ref, ws_ref, bexp_ref, o_ref):
    row = lax.broadcasted_iota(jnp.int32, (GMLP_BLOCK, GMLP_BLOCK), 0) // CHUNK
    col = lax.broadcasted_iota(jnp.int32, (GMLP_BLOCK, GMLP_BLOCK), 1) // CHUNK
    mask = col <= row
    for blk in range(GMLP_STEP_BLOCKS):
        rs = slice(blk * GMLP_BLOCK, (blk + 1) * GMLP_BLOCK)
        u = u_ref[rs, :]
        v = v_ref[rs, :]
        mu = jnp.mean(v, axis=-1, keepdims=True)
        vc = v - mu
        vn = vc * lax.rsqrt(jnp.mean(vc * vc, axis=-1, keepdims=True) + NORM_EPS)
        vn = vn * lng_ref[...] + lnb_ref[...]
        gate = z_ref[rs, :]
        for g in range(GMLP_GROUPS):
            sl = slice(g * GMLP_GROUP_DIM, (g + 1) * GMLP_GROUP_DIM)
            ws = jnp.where(mask, ws_ref[g], 0.0)
            mixed = jnp.dot(ws, vn[:, sl], preferred_element_type=jnp.float32) + bexp_ref[:, sl]
            o_ref[rs, sl] = (u[:, sl] * mixed * gate[:, sl]).astype(o_ref.dtype)


def _gmlp(uv_act, z_act, ln_g, ln_b, w_s, bexp):
    s = uv_act.shape[0]
    w = GMLP_WIDTH
    tr = GMLP_STEP_BLOCKS * GMLP_BLOCK
    assert s % tr == 0
    row = lambda c: pl.BlockSpec((tr, w), lambda i, c=c: (i, c))
    full2 = lambda shp: pl.BlockSpec(shp, lambda i: (0, 0))
    return pl.pallas_call(
        _gmlp_kernel,
        grid=(s // tr,),
        in_specs=[row(0), row(1), row(0), full2((1, w)), full2((1, w)),
                  pl.BlockSpec((GMLP_GROUPS, GMLP_BLOCK, GMLP_BLOCK), lambda i: (0, 0, 0)),
                  full2((GMLP_BLOCK, w))],
        out_specs=pl.BlockSpec((tr, w), lambda i: (i, 0)),
        out_shape=jax.ShapeDtypeStruct((s, w), jnp.bfloat16),
        compiler_params=_params("arbitrary"),
        name="gmlp_gating",
    )(uv_act, uv_act, z_act, ln_g.reshape(1, w), ln_b.reshape(1, w), w_s, bexp)


def _bmm(a, b):
    return jnp.einsum("hij,hjk->hik", a, b, preferred_element_type=jnp.float32)


def _bmm_nt(a, b):
    return jnp.einsum("hid,hjd->hij", a, b, preferred_element_type=jnp.float32)


def _split3(x):
    bf16, f32 = jnp.bfloat16, jnp.float32
    hi = x.astype(bf16)
    r1 = x - hi.astype(f32)
    mid = r1.astype(bf16)
    lo = (r1 - mid.astype(f32)).astype(bf16)
    return hi, mid, lo


def _deltanet_kernel(qkv_ref, ba_ref, z_ref, alog_ref, dtb_ref, ng_ref, o_ref, rows_ref, state_ref):
    f32, bf16 = jnp.float32, jnp.bfloat16
    i = pl.program_id(0)
    nh = DN_HEADS
    nc = DN_STEP_CHUNKS
    rows = nc * CHUNK
    nb = nc * nh

    @pl.when(i == 0)
    def _():
        state_ref[...] = jnp.zeros(state_ref.shape, f32)

    qkv = qkv_ref[...]

    def split(x2d):
        return jnp.stack([x2d[ci * CHUNK:(ci + 1) * CHUNK, h * DN_HEAD_DIM:(h + 1) * DN_HEAD_DIM]
                          for ci in range(nc) for h in range(nh)], axis=0)

    q = split(qkv[:, :DN_WIDTH])
    k = split(qkv[:, DN_WIDTH:2 * DN_WIDTH])
    v = split(qkv[:, 2 * DN_WIDTH:])
    ones = jnp.ones((DN_HEAD_DIM, DN_HEAD_DIM), f32)

    def sumsq(x):
        sq = (x * x).reshape(nb * CHUNK, DN_HEAD_DIM)
        return jnp.dot(sq, ones, preferred_element_type=f32).reshape(nb, CHUNK, DN_HEAD_DIM)

    q = q * lax.rsqrt((sumsq(q) + NORM_EPS) * float(DN_HEAD_DIM))
    k = k * lax.rsqrt(sumsq(k) + NORM_EPS)

    ba = ba_ref[...]
    beta_all = _sigmoid(ba)
    xg = ba + dtb_ref[...]
    softplus = jnp.maximum(xg, 0.0) + jnp.log1p(jnp.exp(-jnp.abs(xg)))
    g_all = -jnp.exp(alog_ref[...]) * softplus
    r = lax.broadcasted_iota(jnp.int32, (CHUNK, CHUNK), 0)
    c = lax.broadcasted_iota(jnp.int32, (CHUNK, CHUNK), 1)
    incl = r >= c
    strict = r > c
    rr = lax.broadcasted_iota(jnp.int32, (rows, rows), 0)
    cc = lax.broadcasted_iota(jnp.int32, (rows, rows), 1)
    tril = ((rr >= cc) & ((rr // CHUNK) == (cc // CHUNK))).astype(bf16)
    gcum_all = jnp.dot(jnp.concatenate([tril, tril, tril], axis=1),
                       jnp.concatenate(_split3(g_all), axis=0), preferred_element_type=f32)

    lane = lax.broadcasted_iota(jnp.int32, (rows, LANES), 1)
    merged = jnp.where(lane < nh, beta_all, gcum_all)
    for ci in range(nc):
        rows_ref[ci] = merged[ci * CHUNK:(ci + 1) * CHUNK, :].T
    pairs = [(ci, h) for ci in range(nc) for h in range(nh)]
    b_row = jnp.stack([rows_ref[ci, h:h + 1, :] for ci, h in pairs], axis=0)
    gc_row = jnp.stack([rows_ref[ci, nh + h:nh + h + 1, :] for ci, h in pairs], axis=0)
    b_col = jnp.stack([jnp.broadcast_to(beta_all[ci * CHUNK:(ci + 1) * CHUNK, h:h + 1], (CHUNK, LANES))
                       for ci, h in pairs], axis=0)
    gc_col = jnp.stack([jnp.broadcast_to(gcum_all[ci * CHUNK:(ci + 1) * CHUNK, nh + h:nh + h + 1], (CHUNK, LANES))
                        for ci, h in pairs], axis=0)

    gam = jnp.exp(gc_col)
    gc_last = gc_col[:, CHUNK - 1:CHUNK, :]
    g_end = jnp.exp(gc_last)
    kdb = jnp.exp(gc_last - gc_col) * b_col
    decb = jnp.exp(gc_col[:, :, :CHUNK] - gc_row) * b_row

    a = jnp.where(strict[None], _bmm_nt(k, k) * decb, 0.0)
    eye_c = (r == c).astype(f32)[None]
    same_blk = ((r // SUBLANES) == (c // SUBLANES))[None]

    def inv_nilpotent8(p):
        pb = (-p).astype(bf16)
        t = eye_c - p
        for _ in range(2):
            pb = _bmm(pb, pb).astype(bf16)
            t = t + _bmm(t.astype(bf16), pb)
        return t

    t_d = inv_nilpotent8(jnp.where(same_blk, a, 0.0))
    t_db = t_d.astype(bf16)
    t_n = inv_nilpotent8(_bmm(t_db, jnp.where(same_blk, 0.0, a).astype(bf16)))
    tinv = _bmm(t_n.astype(bf16), t_db)
    rhs = jnp.concatenate([v, gam * k], axis=-1)
    sol = _bmm(tinv, rhs)
    u_s = sol[:, :, :DN_HEAD_DIM]
    kc_s = sol[:, :, DN_HEAD_DIM:]

    aqkb = jnp.where(incl[None], _bmm_nt(q, k) * decb, 0.0)
    lhs = jnp.concatenate([kc_s, q * gam], axis=1)
    kd = k * kdb

    state = state_ref[...]
    outs = []
    for ci in range(nc):
        sl = slice(ci * nh, (ci + 1) * nh)
        prod = _bmm(lhs[sl], state)
        w_s = u_s[sl] - prod[:, :CHUNK]
        outs.append(prod[:, CHUNK:] + _bmm(aqkb[sl], w_s))
        state = g_end[sl] * state + _bmm(jnp.swapaxes(kd[sl], 1, 2), w_s)
    state_ref[...] = state

    o = jnp.concatenate(outs, axis=0)
    o = o * lax.rsqrt(sumsq(o) * (1.0 / DN_HEAD_DIM) + NORM_EPS) * ng_ref[...]
    o2d = jnp.concatenate([jnp.concatenate([o[ci * nh + h] for h in range(nh)], axis=-1)
                           for ci in range(nc)], axis=0)
    o_ref[...] = (o2d * z_ref[...]).astype(o_ref.dtype)


def _deltanet(qkv_act, ba, z_act, alog_row, dtb_row, ng):
    s = qkv_act.shape[0]
    rows = DN_STEP_CHUNKS * CHUNK
    assert s % rows == 0
    qkv_blk = 3 * DN_WIDTH
    full2 = lambda shp: pl.BlockSpec(shp, lambda i: (0, 0))
    return pl.pallas_call(
        _deltanet_kernel,
        grid=(s // rows,),
        in_specs=[pl.BlockSpec((rows, qkv_blk), lambda i: (i, 0)),
                  pl.BlockSpec((rows, LANES), lambda i: (i, 0)),
                  pl.BlockSpec((rows, DN_WIDTH), lambda i: (i, 0)),
                  full2((1, LANES)), full2((1, LANES)), full2((1, DN_HEAD_DIM))],
        out_specs=pl.BlockSpec((rows, DN_WIDTH), lambda i: (i, 0)),
        out_shape=jax.ShapeDtypeStruct((s, DN_WIDTH), jnp.bfloat16),
        scratch_shapes=[pltpu.VMEM((DN_STEP_CHUNKS, LANES, CHUNK), jnp.float32),
                        pltpu.VMEM((DN_HEADS, DN_HEAD_DIM, DN_HEAD_DIM), jnp.float32)],
        compiler_params=_params("arbitrary"),
        name="gated_deltanet",
    )(qkv_act, ba, z_act, alog_row, dtb_row, ng.reshape(1, DN_HEAD_DIM))


def _merge_kernel(h_ref, agm_ref, adn_ref, wggm_ref, wgdn_ref, wbgm_ref, wbdn_ref, o_ref):
    f32 = jnp.float32
    h = h_ref[...]
    nt = (((1,), (1,)), ((), ()))
    gate_gm = _sigmoid(lax.dot_general(h, wggm_ref[...], nt, preferred_element_type=f32))
    y_gm = jnp.dot(agm_ref[...], wbgm_ref[...], preferred_element_type=f32)
    acc = gate_gm * y_gm
    gate_dn = _sigmoid(lax.dot_general(h, wgdn_ref[...], nt, preferred_element_type=f32))
    y_dn = jnp.dot(adn_ref[...], wbdn_ref[...], preferred_element_type=f32)
    o_ref[...] = (acc + gate_dn * y_dn).astype(o_ref.dtype)


def _merge(h, a_gm, a_dn, w_gate_t, w_br_gm, w_br_dn, tm=1024, tn=256):
    s, d = h.shape
    w = a_gm.shape[1]
    ngate = d // tn
    return pl.pallas_call(
        _merge_kernel,
        grid=(s // tm, d // tn),
        in_specs=[pl.BlockSpec((tm, d), lambda i, j: (i, 0)),
                  pl.BlockSpec((tm, w), lambda i, j: (i, 0)),
                  pl.BlockSpec((tm, w), lambda i, j: (i, 0)),
                  pl.BlockSpec((tn, d), lambda i, j: (j, 0)),
                  pl.BlockSpec((tn, d), lambda i, j: (j + ngate, 0)),
                  pl.BlockSpec((w, tn), lambda i, j: (0, j)),
                  pl.BlockSpec((w, tn), lambda i, j: (0, j))],
        out_specs=pl.BlockSpec((tm, tn), lambda i, j: (i, j)),
        out_shape=jax.ShapeDtypeStruct((s, d), jnp.bfloat16),
        compiler_params=_params("arbitrary", "arbitrary"),
        name="gated_merge",
    )(h, a_gm, a_dn, w_gate_t, w_gate_t, w_br_gm, w_br_dn)


def kernel(x, norm_g, w_in, conv_w, a_log, dt_bias, dn_norm_g, ln_g, ln_b, w_s, b_s,
           w_br_gmlp, w_br_dn, w_out, final_g):
    bsz, seq, d = x.shape
    assert bsz == 1, "the conv halo and the delta-rule state are carried across row tiles of one sequence"
    bf16 = jnp.bfloat16
    xs = x.reshape(bsz * seq, d)
    head_pad = jnp.zeros((DN_HEADS,), jnp.float32)
    tail_pad = jnp.zeros((LANES - N_BA,), jnp.float32)
    w_in_t = jnp.swapaxes(w_in, 1, 2)
    for l in range(DEPTH):
        w_gate_t = _cast_rows(w_in_t, l, N_MAIN + N_BA, 2 * d, "gate_weight_cast")
        alog_row = jnp.concatenate([head_pad, a_log[l], tail_pad]).reshape(1, LANES)
        dtb_row = jnp.concatenate([head_pad, dt_bias[l], tail_pad]).reshape(1, LANES)
        bexp = jnp.repeat(b_s[l].T, GMLP_GROUP_DIM, axis=1)

        h = _rmsnorm(xs, norm_g[l], bf16)
        gw, dw = GMLP_WIDTH, DN_WIDTH
        uv_act = _proj(h, w_in_t, l, 0, 2 * gw, 512, 1024, "in_proj_uv", w_is_nk=True, epilogue="gelu")
        zgm_act = _proj(h, w_in_t, l, 2 * gw, gw, 512, 1024, "in_proj_zgm", w_is_nk=True, epilogue="silu")
        qkv_act = _proj(h, w_in_t, l, 3 * gw, 3 * dw, 1024, 512, "in_proj_qkv", w_is_nk=True,
                        epilogue="conv_silu", conv_w=conv_w[l])
        zdn_act = _proj(h, w_in_t, l, 3 * gw + 3 * dw, dw, 512, 1024, "in_proj_zdn", w_is_nk=True, epilogue="silu")
        ba = _proj(h, w_in_t, l, N_MAIN, LANES, 1024, LANES, "in_proj_ba", w_is_nk=True)
        a_gm = _gmlp(uv_act, zgm_act, ln_g[l], ln_b[l], w_s[l], bexp)
        a_dn = _deltanet(qkv_act, ba, zdn_act, alog_row, dtb_row, dn_norm_g[l])
        merged = _merge(h, a_gm, a_dn, w_gate_t, w_br_gmlp[l].astype(bf16), w_br_dn[l].astype(bf16))
        xs = _proj(merged, w_out, l, 0, d, 512, 1024, "out_proj_residual", residual=xs)
    out = _rmsnorm(xs, final_g, jnp.float32)
    return out.reshape(bsz, seq, d)
```

```python
import functools

import jax
import jax.numpy as jnp
from jax import lax
from jax.experimental import pallas as pl
from jax.experimental.pallas import tpu as pltpu

D_MODEL = 4096
DEPTH = 4
CHUNK = 64
GMLP_BLOCK = 128
GMLP_WIDTH = D_MODEL // 2
GMLP_GROUP_DIM = 128
GMLP_GROUPS = GMLP_WIDTH // GMLP_GROUP_DIM
DN_WIDTH = D_MODEL // 2
DN_HEAD_DIM = 128
DN_HEADS = DN_WIDTH // DN_HEAD_DIM
CONV_WIDTH = 4
NORM_EPS = 1e-6
LANES = 128
SUBLANES = 8
N_MAIN = 3 * GMLP_WIDTH + 4 * DN_WIDTH
N_BA = 2 * DN_HEADS
VMEM_LIMIT = 60 * 1024 * 1024

LOG2_E = 1.4426950408889634
DN_STEP_CHUNKS = 4
GMLP_STEP_BLOCKS = 4


def _params(*sem):
    return pltpu.CompilerParams(dimension_semantics=sem, vmem_limit_bytes=VMEM_LIMIT)


def _sigmoid(x):
    return 1.0 / (1.0 + jnp.exp2(x * (-LOG2_E)))


def _silu(x):
    return x * _sigmoid(x)


def _gelu(x):
    return 0.5 * x * (1.0 + lax.erf(x * (2.0 ** -0.5)))


def _rmsnorm_kernel(x_ref, g_ref, o_ref):
    x = x_ref[...]
    y = x * lax.rsqrt(jnp.mean(x * x, axis=-1, keepdims=True) + NORM_EPS)
    o_ref[...] = (y * g_ref[...]).astype(o_ref.dtype)


def _rmsnorm(x, g, out_dtype, tr=512):
    s, d = x.shape
    return pl.pallas_call(
        _rmsnorm_kernel,
        grid=(s // tr,),
        in_specs=[pl.BlockSpec((tr, d), lambda i: (i, 0)),
                  pl.BlockSpec((1, d), lambda i: (0, 0))],
        out_specs=pl.BlockSpec((tr, d), lambda i: (i, 0)),
        out_shape=jax.ShapeDtypeStruct((s, d), out_dtype),
        compiler_params=_params("arbitrary"),
        name="rmsnorm",
    )(x, g.reshape(1, d))


def _proj_kernel(*refs, has_residual, w_is_nk, epilogue):
    refs = list(refs)
    a_ref, w_ref = refs[:2]
    pos = 2
    x_ref = cw_ref = ext_ref = None
    if has_residual:
        x_ref = refs[pos]
        pos += 1
    if epilogue == "conv_silu":
        cw_ref = refs[pos]
        pos += 1
    o_ref, wb_ref = refs[pos], refs[pos + 1]
    if epilogue == "conv_silu":
        ext_ref = refs[pos + 2]
    halo = SUBLANES

    @pl.when(pl.program_id(1) == 0)
    def _():
        w = w_ref[...]
        wb_ref[...] = (w.T if w_is_nk else w).astype(wb_ref.dtype)
        if ext_ref is not None:
            ext_ref[0:halo, :] = jnp.zeros((halo, ext_ref.shape[1]), jnp.float32)

    acc = jnp.dot(a_ref[...], wb_ref[...], preferred_element_type=jnp.float32)
    if has_residual:
        acc = acc + x_ref[...]
    if epilogue == "gelu":
        acc = _gelu(acc)
    elif epilogue == "silu":
        acc = _silu(acc)
    elif epilogue == "conv_silu":
        tm = acc.shape[0]
        ext_ref[halo:halo + tm, :] = acc
        e = ext_ref[...]
        s1 = pltpu.roll(e, 1, axis=0)
        pa = cw_ref[3:4, :] * e + cw_ref[2:3, :] * s1
        pb = cw_ref[1:2, :] * e + cw_ref[0:1, :] * s1
        conv = (pa + pltpu.roll(pb, 2, axis=0))[halo:, :]
        ext_ref[0:halo, :] = ext_ref[tm:tm + halo, :]
        acc = _silu(conv)
    o_ref[...] = acc.astype(o_ref.dtype)


def _proj(a, w3, layer, col0, n, tm, tn, name, residual=None, w_is_nk=False, epilogue=None, conv_w=None):
    m, k = a.shape
    assert col0 % tn == 0 and n % tn == 0 and m % tm == 0
    cb = col0 // tn
    if w_is_nk:
        w_spec = pl.BlockSpec((None, tn, k), lambda j, i: (layer, j + cb, 0))
    else:
        w_spec = pl.BlockSpec((None, k, tn), lambda j, i: (layer, 0, j + cb))
    in_specs = [pl.BlockSpec((tm, k), lambda j, i: (i, 0)), w_spec]
    args = [a, w3]
    scratch = [pltpu.VMEM((k, tn), jnp.bfloat16)]
    if residual is not None:
        in_specs.append(pl.BlockSpec((tm, tn), lambda j, i: (i, j)))
        args.append(residual)
    if epilogue == "conv_silu":
        in_specs.append(pl.BlockSpec((CONV_WIDTH, tn), lambda j, i: (0, j)))
        args.append(conv_w)
        scratch.append(pltpu.VMEM((SUBLANES + tm, tn), jnp.float32))
    return pl.pallas_call(
        functools.partial(_proj_kernel, has_residual=residual is not None, w_is_nk=w_is_nk, epilogue=epilogue),
        grid=(n // tn, m // tm),
        in_specs=in_specs,
        out_specs=pl.BlockSpec((tm, tn), lambda j, i: (i, j)),
        out_shape=jax.ShapeDtypeStruct((m, n), jnp.float32),
        scratch_shapes=scratch,
        compiler_params=_params("arbitrary", "arbitrary"),
        name=name,
    )(*args)


def _cast_kernel(w_ref, o_ref):
    o_ref[...] = w_ref[0].astype(o_ref.dtype)


def _cast_rows(w3, layer, row0, nrows, name, tr=512):
    k = w3.shape[2]
    assert row0 % (2 * SUBLANES) == 0 and nrows % tr == 0
    return pl.pallas_call(
        _cast_kernel,
        grid=(nrows // tr,),
        in_specs=[pl.BlockSpec((pl.Element(1), pl.Element(tr), pl.Element(k)),
                               lambda i: (layer, pl.multiple_of(row0 + i * tr, 2 * SUBLANES), 0))],
        out_specs=pl.BlockSpec((tr, k), lambda i: (i, 0)),
        out_shape=jax.ShapeDtypeStruct((nrows, k), jnp.bfloat16),
        compiler_params=_params("arbitrary"),
        name=name,
    )(w3)


def _gmlp_kernel(u_ref, v_ref, z_ref, lng_ref, lnb_ref, ws_ref, bexp_ref, o_ref):
    row = lax.broadcasted_iota(jnp.int32, (GMLP_BLOCK, GMLP_BLOCK), 0) // CHUNK
    col = lax.broadcasted_iota(jnp.int32, (GMLP_BLOCK, GMLP_BLOCK), 1) // CHUNK
    mask = col <= row
    for blk in range(GMLP_STEP_BLOCKS):
        rs = slice(blk * GMLP_BLOCK, (blk + 1) * GMLP_BLOCK)
        u = u_ref[rs, :]
        v = v_ref[rs, :]
        mu = jnp.mean(v, axis=-1, keepdims=True)
        vc = v - mu
        vn = vc * lax.rsqrt(jnp.mean(vc * vc, axis=-1, keepdims=True) + NORM_EPS)
        vn = vn * lng_ref[...] + lnb_ref[...]
        gate = z_ref[rs, :]
        for g in range(GMLP_GROUPS):
            sl = slice(g * GMLP_GROUP_DIM, (g + 1) * GMLP_GROUP_DIM)
            ws = jnp.where(mask, ws_ref[g], 0.0)
            mixed = jnp.dot(ws, vn[:, sl], preferred_element_type=jnp.float32) + bexp_ref[:, sl]
            o_ref[rs, sl] = (u[:, sl] * mixed * gate[:, sl]).astype(o_ref.dtype)


def _gmlp(uv_act, z_act, ln_g, ln_b, w_s, bexp):
    s = uv_act.shape[0]
    w = GMLP_WIDTH
    tr = GMLP_STEP_BLOCKS * GMLP_BLOCK
    assert s % tr == 0
    row = lambda c: pl.BlockSpec((tr, w), lambda i, c=c: (i, c))
    full2 = lambda shp: pl.BlockSpec(shp, lambda i: (0, 0))
    return pl.pallas_call(
        _gmlp_kernel,
        grid=(s // tr,),
        in_specs=[row(0), row(1), row(0), full2((1, w)), full2((1, w)),
                  pl.BlockSpec((GMLP_GROUPS, GMLP_BLOCK, GMLP_BLOCK), lambda i: (0, 0, 0)),
                  full2((GMLP_BLOCK, w))],
        out_specs=pl.BlockSpec((tr, w), lambda i: (i, 0)),
        out_shape=jax.ShapeDtypeStruct((s, w), jnp.bfloat16),
        compiler_params=_params("arbitrary"),
        name="gmlp_gating",
    )(uv_act, uv_act, z_act, ln_g.reshape(1, w), ln_b.reshape(1, w), w_s, bexp)


def _bmm(a, b):
    return jnp.einsum("hij,hjk->hik", a, b, preferred_element_type=jnp.float32)


def _bmm_nt(a, b):
    return jnp.einsum("hid,hjd->hij", a, b, preferred_element_type=jnp.float32)


def _split3(x):
    bf16, f32 = jnp.bfloat16, jnp.float32
    hi = x.astype(bf16)
    r1 = x - hi.astype(f32)
    mid = r1.astype(bf16)
    lo = (r1 - mid.astype(f32)).astype(bf16)
    return hi, mid, lo


def _deltanet_kernel(qkv_ref, ba_ref, z_ref, alog_ref, dtb_ref, ng_ref, o_ref, rows_ref, state_ref):
    f32, bf16 = jnp.float32, jnp.bfloat16
    i = pl.program_id(0)
    nh = DN_HEADS
    nc = DN_STEP_CHUNKS
    rows = nc * CHUNK
    nb = nc * nh

    @pl.when(i == 0)
    def _():
        state_ref[...] = jnp.zeros(state_ref.shape, f32)

    qkv = qkv_ref[...]

    def split(x2d):
        return jnp.stack([x2d[ci * CHUNK:(ci + 1) * CHUNK, h * DN_HEAD_DIM:(h + 1) * DN_HEAD_DIM]
                          for ci in range(nc) for h in range(nh)], axis=0)

    q = split(qkv[:, :DN_WIDTH])
    k = split(qkv[:, DN_WIDTH:2 * DN_WIDTH])
    v = split(qkv[:, 2 * DN_WIDTH:])
    ones = jnp.ones((DN_HEAD_DIM, DN_HEAD_DIM), f32)

    def sumsq(x):
        sq = (x * x).reshape(nb * CHUNK, DN_HEAD_DIM)
        return jnp.dot(sq, ones, preferred_element_type=f32).reshape(nb, CHUNK, DN_HEAD_DIM)

    q = q * lax.rsqrt((sumsq(q) + NORM_EPS) * float(DN_HEAD_DIM))
    k = k * lax.rsqrt(sumsq(k) + NORM_EPS)

    ba = ba_ref[...]
    beta_all = _sigmoid(ba)
    xg = ba + dtb_ref[...]
    softplus = jnp.maximum(xg, 0.0) + jnp.log1p(jnp.exp(-jnp.abs(xg)))
    g_all = -jnp.exp(alog_ref[...]) * softplus
    r = lax.broadcasted_iota(jnp.int32, (CHUNK, CHUNK), 0)
    c = lax.broadcasted_iota(jnp.int32, (CHUNK, CHUNK), 1)
    incl = r >= c
    strict = r > c
    rr = lax.broadcasted_iota(jnp.int32, (rows, rows), 0)
    cc = lax.broadcasted_iota(jnp.int32, (rows, rows), 1)
    tril = ((rr >= cc) & ((rr // CHUNK) == (cc // CHUNK))).astype(bf16)
    gcum_all = jnp.dot(jnp.concatenate([tril, tril, tril], axis=1),
                       jnp.concatenate(_split3(g_all), axis=0), preferred_element_type=f32)

    lane = lax.broadcasted_iota(jnp.int32, (rows, LANES), 1)
    merged = jnp.where(lane < nh, beta_all, gcum_all)
    for ci in range(nc):
        rows_ref[ci] = merged[ci * CHUNK:(ci + 1) * CHUNK, :].T
    pairs = [(ci, h) for ci in range(nc) for h in range(nh)]
    b_row = jnp.stack([rows_ref[ci, h:h + 1, :] for ci, h in pairs], axis=0)
    gc_row = jnp.stack([rows_ref[ci, nh + h:nh + h + 1, :] for ci, h in pairs], axis=0)
    b_col = jnp.stack([jnp.broadcast_to(beta_all[ci * CHUNK:(ci + 1) * CHUNK, h:h + 1], (CHUNK, LANES))
                       for ci, h in pairs], axis=0)
    gc_col = jnp.stack([jnp.broadcast_to(gcum_all[ci * CHUNK:(ci + 1) * CHUNK, nh + h:nh + h + 1], (CHUNK, LANES))
                        for ci, h in pairs], axis=0)

    gam = jnp.exp(gc_col)
    gc_last = gc_col[:, CHUNK - 1:CHUNK, :]
    g_end = jnp.exp(gc_last)
    kdb = jnp.exp(gc_last - gc_col) * b_col
    decb = jnp.exp(gc_col[:, :, :CHUNK] - gc_row) * b_row

    a = jnp.where(strict[None], _bmm_nt(k, k) * decb, 0.0)
    eye_c = (r == c).astype(f32)[None]
    same_blk = ((r // SUBLANES) == (c // SUBLANES))[None]

    def inv_nilpotent8(p):
        pb = (-p).astype(bf16)
        t = eye_c - p
        for _ in range(2):
            pb = _bmm(pb, pb).astype(bf16)
            t = t + _bmm(t.astype(bf16), pb)
        return t

    t_d = inv_nilpotent8(jnp.where(same_blk, a, 0.0))
    t_db = t_d.astype(bf16)
    t_n = inv_nilpotent8(_bmm(t_db, jnp.where(same_blk, 0.0, a).astype(bf16)))
    tinv = _bmm(t_n.astype(bf16), t_db)
    rhs = jnp.concatenate([v, gam * k], axis=-1)
    sol = _bmm(tinv, rhs)
    u_s = sol[:, :, :DN_HEAD_DIM]
    kc_s = sol[:, :, DN_HEAD_DIM:]

    aqkb = jnp.where(incl[None], _bmm_nt(q, k) * decb, 0.0)
    lhs = jnp.concatenate([kc_s, q * gam], axis=1)
    kd = k * kdb

    state = state_ref[...]
    outs = []
    for ci in range(nc):
        sl = slice(ci * nh, (ci + 1) * nh)
        prod = _bmm(lhs[sl], state)
        w_s = u_s[sl] - prod[:, :CHUNK]
        outs.append(prod[:, CHUNK:] + _bmm(aqkb[sl], w_s))
        state = g_end[sl] * state + _bmm(jnp.swapaxes(kd[sl], 1, 2), w_s)
    state_ref[...] = state

    o = jnp.concatenate(outs, axis=0)
    o = o * lax.rsqrt(sumsq(o) * (1.0 / DN_HEAD_DIM) + NORM_EPS) * ng_ref[...]
    o2d = jnp.concatenate([jnp.concatenate([o[ci * nh + h] for h in range(nh)], axis=-1)
                           for ci in range(nc)], axis=0)
    o_ref[...] = (o2d * z_ref[...]).astype(o_ref.dtype)


def _deltanet(qkv_act, ba, z_act, alog_row, dtb_row, ng):
    s = qkv_act.shape[0]
    rows = DN_STEP_CHUNKS * CHUNK
    assert s % rows == 0
    qkv_blk = 3 * DN_WIDTH
    full2 = lambda shp: pl.BlockSpec(shp, lambda i: (0, 0))
    return pl.pallas_call(
        _deltanet_kernel,
        grid=(s // rows,),
        in_specs=[pl.BlockSpec((rows, qkv_blk), lambda i: (i, 0)),
                  pl.BlockSpec((rows, LANES), lambda i: (i, 0)),
                  pl.BlockSpec((rows, DN_WIDTH), lambda i: (i, 0)),
                  full2((1, LANES)), full2((1, LANES)), full2((1, DN_HEAD_DIM))],
        out_specs=pl.BlockSpec((rows, DN_WIDTH), lambda i: (i, 0)),
        out_shape=jax.ShapeDtypeStruct((s, DN_WIDTH), jnp.bfloat16),
        scratch_shapes=[pltpu.VMEM((DN_STEP_CHUNKS, LANES, CHUNK), jnp.float32),
                        pltpu.VMEM((DN_HEADS, DN_HEAD_DIM, DN_HEAD_DIM), jnp.float32)],
        compiler_params=_params("arbitrary"),
        name="gated_deltanet",
    )(qkv_act, ba, z_act, alog_row, dtb_row, ng.reshape(1, DN_HEAD_DIM))


def _merge_kernel(h_ref, agm_ref, adn_ref, wggm_ref, wgdn_ref, wbgm_ref, wbdn_ref, o_ref):
    f32 = jnp.float32
    h = h_ref[...]
    nt = (((1,), (1,)), ((), ()))
    gate_gm = _sigmoid(lax.dot_general(h, wggm_ref[...], nt, preferred_element_type=f32))
    y_gm = jnp.dot(agm_ref[...], wbgm_ref[...], preferred_element_type=f32)
    acc = gate_gm * y_gm
    gate_dn = _sigmoid(lax.dot_general(h, wgdn_ref[...], nt, preferred_element_type=f32))
    y_dn = jnp.dot(adn_ref[...], wbdn_ref[...], preferred_element_type=f32)
    o_ref[...] = (acc + gate_dn * y_dn).astype(o_ref.dtype)


def _merge(h, a_gm, a_dn, w_gate_t, w_br_gm, w_br_dn, tm=1024, tn=256):
    s, d = h.shape
    w = a_gm.shape[1]
    ngate = d // tn
    return pl.pallas_call(
        _merge_kernel,
        grid=(s // tm, d // tn),
        in_specs=[pl.BlockSpec((tm, d), lambda i, j: (i, 0)),
                  pl.BlockSpec((tm, w), lambda i, j: (i, 0)),
                  pl.BlockSpec((tm, w), lambda i, j: (i, 0)),
                  pl.BlockSpec((tn, d), lambda i, j: (j, 0)),
                  pl.BlockSpec((tn, d), lambda i, j: (j + ngate, 0)),
                  pl.BlockSpec((w, tn), lambda i, j: (0, j)),
                  pl.BlockSpec((w, tn), lambda i, j: (0, j))],
        out_specs=pl.BlockSpec((tm, tn), lambda i, j: (i, j)),
        out_shape=jax.ShapeDtypeStruct((s, d), jnp.bfloat16),
        compiler_params=_params("arbitrary", "arbitrary"),
        name="gated_merge",
    )(h, a_gm, a_dn, w_gate_t, w_gate_t, w_br_gm, w_br_dn)


def kernel(x, norm_g, w_in, conv_w, a_log, dt_bias, dn_norm_g, ln_g, ln_b, w_s, b_s,
           w_br_gmlp, w_br_dn, w_out, final_g):
    bsz, seq, d = x.shape
    assert bsz == 1, "the conv halo and the delta-rule state are carried across row tiles of one sequence"
    bf16 = jnp.bfloat16
    xs = x.reshape(bsz * seq, d)
    head_pad = jnp.zeros((DN_HEADS,), jnp.float32)
    tail_pad = jnp.zeros((LANES - N_BA,), jnp.float32)
    w_in_t = jnp.swapaxes(w_in, 1, 2)
    for l in range(DEPTH):
        w_gate_t = _cast_rows(w_in_t, l, N_MAIN + N_BA, 2 * d, "gate_weight_cast")
        alog_row = jnp.concatenate([head_pad, a_log[l], tail_pad]).reshape(1, LANES)
        dtb_row = jnp.concatenate([head_pad, dt_bias[l], tail_pad]).reshape(1, LANES)
        bexp = jnp.repeat(b_s[l].T, GMLP_GROUP_DIM, axis=1)

        h = _rmsnorm(xs, norm_g[l], bf16)
        gw, dw = GMLP_WIDTH, DN_WIDTH
        uv_act = _proj(h, w_in_t, l, 0, 2 * gw, 512, 1024, "in_proj_uv", w_is_nk=True, epilogue="gelu")
        zgm_act = _proj(h, w_in_t, l, 2 * gw, gw, 512, 1024, "in_proj_zgm", w_is_nk=True, epilogue="silu")
        qkv_act = _proj(h, w_in_t, l, 3 * gw, 3 * dw, 1024, 512, "in_proj_qkv", w_is_nk=True,
                        epilogue="conv_silu", conv_w=conv_w[l])
        zdn_act = _proj(h, w_in_t, l, 3 * gw + 3 * dw, dw, 512, 1024, "in_proj_zdn", w_is_nk=True, epilogue="silu")
        ba = _proj(h, w_in_t, l, N_MAIN, LANES, 1024, LANES, "in_proj_ba", w_is_nk=True)
        a_gm = _gmlp(uv_act, zgm_act, ln_g[l], ln_b[l], w_s[l], bexp)
        a_dn = _deltanet(qkv_act, ba, zdn_act, alog_row, dtb_row, dn_norm_g[l])
        merged = _merge(h, a_gm, a_dn, w_gate_t, w_br_gmlp[l].astype(bf16), w_br_dn[l].astype(bf16))
        xs = _proj(merged, w_out, l, 0, d, 512, 1024, "out_proj_residual", residual=xs)
    out = _rmsnorm(xs, final_g, jnp.float32)
    return out.reshape(bsz, seq, d)
```
